```python
import math
import jax, jax.numpy as jnp
from jax import lax
import numpy as np

D_MODEL = 1024
BATCH = 2
SEQ = 16384
DEPTH = 1

GRID_W = 64
CTX_LEN = 256
N_ADA = 9
FFN_HIDDEN = 2816
D_HYENA = 512
HYENA_N_PROJ = 3
SHORT_CONV_W = 3
FILTER_EMB_BANDS = 16
FILTER_EMB_DIM = 1 + 2 * FILTER_EMB_BANDS
FILTER_HIDDEN = 64
FILTER_INNER = 2
FILTER_OUT_SCALE = 0.02 * FILTER_HIDDEN ** -0.5
DECAY_TARGET = 1e-2
FAST_DECAY_PCT = 0.3
SLOW_DECAY_PCT = 1.5
N_DIFF_HEADS = 4
DIFF_HEAD_DIM = 64
D_DIFF = N_DIFF_HEADS * 2 * DIFF_HEAD_DIM
D_MIX = D_HYENA + D_DIFF
D_IN = HYENA_N_PROJ * D_HYENA + 3 * D_DIFF
ROPE_BASE = 10000.0
Q_BLOCK = 128
RMS_EPS = 1e-6
SUBLN_EPS = 1e-5

kernel_name = 'hybrid_hyena_diffattn_dit_layer'


def rms_norm(x, g, eps=RMS_EPS):
    xf = x.astype(jnp.float32)
    y = xf * lax.rsqrt(jnp.mean(xf * xf, axis=-1, keepdims=True) + eps)
    return (y * g.astype(jnp.float32)).astype(x.dtype)


def adaln(x, g, shift, scale):
    return rms_norm(x, g) * (1.0 + scale) + shift


def ffn_half_step(s, g, shift, scale, gate, w_gate, w_up, w_down):
    u = adaln(s, g, shift, scale)
    return s + 0.5 * gate * ((jax.nn.silu(u @ w_gate) * (u @ w_up)) @ w_down)


def centred_short_conv(z, w, b):
    zp = jnp.pad(z, ((0, 0), (1, 1), (0, 0)))
    return zp[:, :-2] * w[0] + zp[:, 1:-1] * w[1] + zp[:, 2:] * w[2] + b


def hyena_filter(L, w1, b1, w_inner, b_inner, freq, w_out):
    t = jnp.linspace(0.0, 1.0, L, dtype=jnp.float32)[:, None]
    omega = (2.0 * math.pi / L) * jnp.arange(L, dtype=jnp.float32)[:, None]
    bands = jnp.linspace(1e-4, FILTER_EMB_BANDS - 1, FILTER_EMB_BANDS, dtype=jnp.float32)[None, :]
    ang = omega * bands
    emb = jnp.concatenate([t, jnp.cos(ang), -jnp.sin(ang)], axis=-1)
    freq = freq.astype(jnp.float32)
    h = jnp.sin(freq * (emb @ w1.astype(jnp.float32) + b1.astype(jnp.float32)))
    for i in range(FILTER_INNER):
        h = jnp.sin(freq * (h @ w_inner[i].astype(jnp.float32) + b_inner[i].astype(jnp.float32)))
    h = (h @ w_out.astype(jnp.float32)).reshape(L, 2, D_HYENA)
    min_decay = math.log(DECAY_TARGET) / SLOW_DECAY_PCT
    max_decay = math.log(DECAY_TARGET) / FAST_DECAY_PCT
    deltas = jnp.abs(jnp.linspace(min_decay, max_decay, D_HYENA, dtype=jnp.float32))
    return h * jnp.exp(-t * deltas)[:, None, :]


def bidirectional_fft_conv(u, h, bias):
    L = u.shape[1]
    h_fwd, h_bwd = h[:, 0], h[:, 1]
    k_full = jnp.concatenate([h_fwd, jnp.zeros_like(h_fwd[:1]), h_bwd[:0:-1]], axis=0)
    k_f = jnp.fft.rfft(k_full, n=2 * L, axis=0)
    uf = u.astype(jnp.float32)
    u_f = jnp.fft.rfft(uf, n=2 * L, axis=1)
    y = jnp.fft.irfft(u_f * k_f[None], n=2 * L, axis=1)[:, :L]
    return (y + uf * bias.astype(jnp.float32)).astype(u.dtype)


def hyena_mixer(z, conv_w, conv_b, w1, b1, w_inner, b_inner, freq, w_out, bias):
    L = z.shape[1]
    z = centred_short_conv(z, conv_w, conv_b)
    x0, x1, v = jnp.split(z, HYENA_N_PROJ, axis=-1)
    h = hyena_filter(L, w1, b1, w_inner, b_inner, freq, w_out)
    return x0 * bidirectional_fft_conv(x1 * v, h, bias)


def axial_rope_tables(n_lat):
    rows = n_lat // GRID_W
    row = jnp.broadcast_to(jnp.arange(rows, dtype=jnp.float32)[:, None], (rows, GRID_W)).reshape(-1)
    col = jnp.broadcast_to(jnp.arange(GRID_W, dtype=jnp.float32)[None, :], (rows, GRID_W)).reshape(-1)
    axis_dim = DIFF_HEAD_DIM // 2
    inv_freq = ROPE_BASE ** (-jnp.arange(0, axis_dim, 2, dtype=jnp.float32) / axis_dim)
    ang = jnp.stack([row[:, None] * inv_freq, col[:, None] * inv_freq], axis=1)
    return jnp.cos(ang), jnp.sin(ang)


def apply_axial_rope(x, cos, sin):
    b_, n_, h_, s_, d_ = x.shape
    xr = x.astype(jnp.float32).reshape(b_, n_, h_, s_, 2, 2, d_ // 4)
    x1, x2 = xr[..., 0, :], xr[..., 1, :]
    cs, sn = cos[None, :, None, None], sin[None, :, None, None]
    out = jnp.stack([x1 * cs - x2 * sn, x2 * cs + x1 * sn], axis=-2)
    return out.reshape(x.shape).astype(x.dtype)


def diff_attention(q, k, v, lam):
    s = jnp.einsum('bqhsd,bkhsd->bhsqk', q, k).astype(jnp.float32) * (DIFF_HEAD_DIM ** -0.5)
    p = jax.nn.softmax(s, axis=-1)
    a = p[:, :, 0] - lam * p[:, :, 1]
    return jnp.einsum('bhqk,bkhe->bqhe', a.astype(v.dtype), v)


def diff_head_out(o, g, lam_init):
    o = rms_norm(o, g, SUBLN_EPS) * (1.0 - lam_init)
    return o.reshape(o.shape[0], o.shape[1], D_DIFF)


def setup_inputs(seed: int = 0) -> dict:
    key = jax.random.key(seed)
    ks = jax.random.split(key, 24)

    def nrm(k, shape, s):
        return s * jax.random.normal(k, shape, jnp.float32)

    return {
        'x': nrm(ks[0], (BATCH, SEQ, D_MODEL), 1.0),
        'c': nrm(ks[1], (BATCH, D_MODEL), 1.0),
        'ctx': nrm(ks[2], (BATCH, CTX_LEN, D_MODEL), 1.0),
        'c_ctx': nrm(ks[3], (D_MODEL,), 1.0),
        'ada_w': nrm(ks[4], (DEPTH, D_MODEL, N_ADA * D_MODEL), 0.5 * D_MODEL ** -0.5),
        'ada_b': nrm(ks[5], (DEPTH, N_ADA * D_MODEL), 0.01),
        'norm_g': 1.0 + nrm(ks[6], (DEPTH, 3, D_MODEL), 0.02),
        'ffn_w_gate': nrm(ks[7], (DEPTH, 2, D_MODEL, FFN_HIDDEN), D_MODEL ** -0.5),
        'ffn_w_up': nrm(ks[8], (DEPTH, 2, D_MODEL, FFN_HIDDEN), D_MODEL ** -0.5),
        'ffn_w_down': nrm(ks[9], (DEPTH, 2, FFN_HIDDEN, D_MODEL), FFN_HIDDEN ** -0.5),
        'w_in': nrm(ks[10], (DEPTH, D_MODEL, D_IN), D_MODEL ** -0.5),
        'w_out': nrm(ks[11], (DEPTH, D_MIX, D_MODEL), D_MIX ** -0.5),
        'hyena_conv_w': nrm(ks[12], (DEPTH, SHORT_CONV_W, HYENA_N_PROJ * D_HYENA), SHORT_CONV_W ** -0.5),
        'hyena_conv_b': nrm(ks[13], (DEPTH, HYENA_N_PROJ * D_HYENA), 0.01),
        'filt_w1': nrm(ks[14], (DEPTH, FILTER_EMB_DIM, FILTER_HIDDEN), FILTER_EMB_DIM ** -0.5),
        'filt_b1': nrm(ks[15], (DEPTH, FILTER_HIDDEN), 0.1),
        'filt_w_inner': nrm(ks[16], (DEPTH, FILTER_INNER, FILTER_HIDDEN, FILTER_HIDDEN), FILTER_HIDDEN ** -0.5),
        'filt_b_inner': nrm(ks[17], (DEPTH, FILTER_INNER, FILTER_HIDDEN), 0.1),
        'filt_sin_freq': 1.0 + nrm(ks[18], (DEPTH, FILTER_HIDDEN), 0.02),
        'filt_w_out': nrm(ks[19], (DEPTH, FILTER_HIDDEN, 2 * D_HYENA), FILTER_OUT_SCALE),
        'hyena_bias': nrm(ks[20], (DEPTH, D_HYENA), 1.0),
        'diff_lambda': nrm(ks[21], (DEPTH, 4, DIFF_HEAD_DIM), 0.1),
        'diff_subln_g': 1.0 + nrm(ks[22], (DEPTH, 2 * DIFF_HEAD_DIM), 0.02),
        'final_g': 1.0 + nrm(ks[23], (D_MODEL,), 0.02),
    }


def reference(x, c, ctx, c_ctx, ada_w, ada_b, norm_g, ffn_w_gate, ffn_w_up, ffn_w_down, w_in, w_out,
              hyena_conv_w, hyena_conv_b, filt_w1, filt_b1, filt_w_inner, filt_b_inner, filt_sin_freq,
              filt_w_out, hyena_bias, diff_lambda, diff_subln_g, final_g):
    B, N, D = x.shape
    C = ctx.shape[1]
    H, d = N_DIFF_HEADS, DIFF_HEAD_DIM
    n_blk = N // Q_BLOCK
    hy_end = HYENA_N_PROJ * D_HYENA
    q_end = hy_end + D_DIFF
    cos, sin = axial_rope_tables(N)
    h_lat, h_ctx = x, ctx
    for layer in range(DEPTH):
        last = layer == DEPTH - 1
        lam_init = 0.8 - 0.6 * math.exp(-0.3 * layer)
        mod_lat = (jax.nn.silu(c) @ ada_w[layer] + ada_b[layer]).reshape(B, N_ADA, D).transpose(1, 0, 2)[:, :, None, :]
        mod_ctx = (jax.nn.silu(c_ctx) @ ada_w[layer] + ada_b[layer]).reshape(N_ADA, 1, 1, D)
        hyena_params = (hyena_conv_w[layer], hyena_conv_b[layer], filt_w1[layer], filt_b1[layer],
                        filt_w_inner[layer], filt_b_inner[layer], filt_sin_freq[layer], filt_w_out[layer],
                        hyena_bias[layer])

        ffn0 = (ffn_w_gate[layer, 0], ffn_w_up[layer, 0], ffn_w_down[layer, 0])
        h_lat = ffn_half_step(h_lat, norm_g[layer, 0], mod_lat[0], mod_lat[1], mod_lat[2], *ffn0)
        h_ctx = ffn_half_step(h_ctx, norm_g[layer, 0], mod_ctx[0], mod_ctx[1], mod_ctx[2], *ffn0)

        u_lat = adaln(h_lat, norm_g[layer, 1], mod_lat[3], mod_lat[4])
        u_ctx = adaln(h_ctx, norm_g[layer, 1], mod_ctx[3], mod_ctx[4])
        p_lat = u_lat @ w_in[layer]
        kv_ctx = u_ctx @ w_in[layer][:, q_end:]
        lp = diff_lambda[layer].astype(jnp.float32)
        lam = jnp.exp(jnp.sum(lp[0] * lp[1])) - jnp.exp(jnp.sum(lp[2] * lp[3])) + lam_init

        q_lat = apply_axial_rope(p_lat[..., hy_end:q_end].reshape(B, N, H, 2, d), cos, sin)
        k_lat = apply_axial_rope(p_lat[..., q_end:q_end + D_DIFF].reshape(B, N, H, 2, d), cos, sin)
        v_lat = p_lat[..., q_end + D_DIFF:].reshape(B, N, H, 2 * d)
        k_ctx = kv_ctx[..., :D_DIFF].reshape(B, C, H, 2, d)
        v_ctx = kv_ctx[..., D_DIFF:].reshape(B, C, H, 2 * d)
        k_all = jnp.concatenate([k_lat, k_ctx], axis=1)
        v_all = jnp.concatenate([v_lat, v_ctx], axis=1)
        q_blocks = q_lat.reshape(B, n_blk, Q_BLOCK, H, 2, d).swapaxes(0, 1)
        o_lat = lax.map(lambda qb: diff_attention(qb, k_all, v_all, lam), q_blocks)
        o_lat = o_lat.swapaxes(0, 1).reshape(B, N, H, 2 * d)
        y_lat = jnp.concatenate([hyena_mixer(p_lat[..., :hy_end], *hyena_params),
                                 diff_head_out(o_lat, diff_subln_g[layer], lam_init)], axis=-1)
        h_lat = h_lat + mod_lat[5] * (y_lat @ w_out[layer])

        ffn1 = (ffn_w_gate[layer, 1], ffn_w_up[layer, 1], ffn_w_down[layer, 1])
        if not last:
            p_ctx = u_ctx @ w_in[layer][:, :q_end]
            q_ctx = p_ctx[..., hy_end:].reshape(B, C, H, 2, d)
            o_ctx = diff_attention(q_ctx, k_ctx, v_ctx, lam)
            y_ctx = jnp.concatenate([hyena_mixer(p_ctx[..., :hy_end], *hyena_params),
                                     diff_head_out(o_ctx, diff_subln_g[layer], lam_init)], axis=-1)
            h_ctx = h_ctx + mod_ctx[5] * (y_ctx @ w_out[layer])
            h_ctx = ffn_half_step(h_ctx, norm_g[layer, 2], mod_ctx[6], mod_ctx[7], mod_ctx[8], *ffn1)

        h_lat = ffn_half_step(h_lat, norm_g[layer, 2], mod_lat[6], mod_lat[7], mod_lat[8], *ffn1)
    return rms_norm(h_lat, final_g)
```

```python
import functools
import math

import numpy as np
import jax
import jax.numpy as jnp
from jax import lax
from jax.experimental import pallas as pl
from jax.experimental.pallas import tpu as pltpu

F32 = jnp.float32
BF16 = jnp.bfloat16

N_ADA = 9
FFN_HIDDEN = 2816
D_HYENA = 512
HYENA_N_PROJ = 3
FILTER_EMB_BANDS = 16
FILTER_HIDDEN = 64
FILTER_INNER = 2
DECAY_TARGET = 1e-2
FAST_DECAY_PCT = 0.3
SLOW_DECAY_PCT = 1.5
N_DIFF_HEADS = 4
DIFF_HEAD_DIM = 64
D_DIFF = N_DIFF_HEADS * 2 * DIFF_HEAD_DIM
GRID_W = 64
ROPE_BASE = 10000.0
RMS_EPS = 1e-6
SUBLN_EPS = 1e-5
LAM_INIT = 0.8 - 0.6 * math.exp(-0.3 * 0)

LANES = 128
V7X_VMEM_BYTES = 64 * 2**20
VMEM_LIMIT = 56 * 2**20

FFN_TM = 512
FFN_TF = 256
PROJ_TM = 512
ATT_TQ = 512
FFT_M2 = 128
FFT_TN = 4096
FFT_KC = 8
FILT_TL = 1024
NEG_BIG = -1e30

LOG2E = 1.4426950408889634


def _cparams(*sem):
    return pltpu.CompilerParams(dimension_semantics=sem, vmem_limit_bytes=VMEM_LIMIT)


def _dot(a, b):
    return jnp.dot(a, b, preferred_element_type=F32)


def _dot3(a, b):
    a_hi = a.astype(BF16)
    a_lo = (a - a_hi.astype(F32)).astype(BF16)
    b_hi = b.astype(BF16)
    b_lo = (b - b_hi.astype(F32)).astype(BF16)
    return _dot(a_hi, b_hi) + (_dot(a_lo, b_hi) + _dot(a_hi, b_lo))


def _silu(x):
    return x * (1.0 / (1.0 + jnp.exp(-x)))


def _adaln(s, g, shift, scale):
    ms = jnp.mean(s * s, axis=-1, keepdims=True)
    return (s * lax.rsqrt(ms + RMS_EPS) * g) * (1.0 + scale) + shift


def _mod_kernel(c_ref, w_ref, b_ref, o_ref):
    o_ref[...] = _dot3(_silu(c_ref[...]), w_ref[...]) + b_ref[...]


def _modulation(cc, ada_w, ada_b):
    rows, d = cc.shape
    n = ada_w.shape[1]
    tn = 768
    return pl.pallas_call(
        _mod_kernel,
        grid=(n // tn,),
        in_specs=[pl.BlockSpec((rows, d), lambda j: (0, 0)),
                  pl.BlockSpec((d, tn), lambda j: (0, j)),
                  pl.BlockSpec((1, tn), lambda j: (0, j))],
        out_specs=pl.BlockSpec((rows, tn), lambda j: (0, j)),
        out_shape=jax.ShapeDtypeStruct((rows, n), F32),
        compiler_params=_cparams("parallel"),
        name="modulation",
    )(cc, ada_w, ada_b.reshape(1, n))


def _mod_spec(arr, nb):
    d = arr.shape[-1]
    if arr.shape[0] == nb:
        return pl.BlockSpec((1, 1, d), lambda b, i: (b, 0, 0))
    return pl.BlockSpec((1, 1, d), lambda b, i: (0, 0, 0))


def _const_spec(arr):
    nd = arr.ndim
    return pl.BlockSpec(arr.shape, lambda b, i: (0,) * nd)


def _ffn_kernel(s_ref, g_ref, sh_ref, sc_ref, gt_ref, wg_ref, wu_ref, wd_ref, *rest, n_chunks, final):
    if final:
        fg_ref, o_ref = rest
    else:
        (o_ref,) = rest
    s = s_ref[0]
    ub = _adaln(s, g_ref[...], sh_ref[0], sc_ref[0]).astype(BF16)
    acc = jnp.zeros(s.shape, F32)
    for f in range(n_chunks):
        gate = _dot(ub, wg_ref[f])
        up = _dot(ub, wu_ref[f])
        a = (_silu(gate) * up).astype(BF16)
        acc = acc + _dot(a, wd_ref[f])
    out = s + (0.5 * gt_ref[0]) * acc
    if final:
        ms = jnp.mean(out * out, axis=-1, keepdims=True)
        out = out * lax.rsqrt(ms + RMS_EPS) * fg_ref[...]
    o_ref[0] = out


def _ffn_half_step(s, g, shift, scale, gate, wg3, wu3, wd3, final_g=None):
    nb, t, d = s.shape
    tm = min(FFN_TM, t)
    n_chunks = wg3.shape[0]
    final = final_g is not None
    args = [s, g.reshape(1, d), shift, scale, gate, wg3, wu3, wd3]
    in_specs = [pl.BlockSpec((1, tm, d), lambda b, i: (b, i, 0)),
                _const_spec(args[1]),
                _mod_spec(shift, nb), _mod_spec(scale, nb), _mod_spec(gate, nb),
                _const_spec(wg3), _const_spec(wu3), _const_spec(wd3)]
    if final:
        args.append(final_g.reshape(1, d))
        in_specs.append(_const_spec(args[-1]))
    return pl.pallas_call(
        functools.partial(_ffn_kernel, n_chunks=n_chunks, final=final),
        grid=(nb, t // tm),
        in_specs=in_specs,
        out_specs=pl.BlockSpec((1, tm, d), lambda b, i: (b, i, 0)),
        out_shape=jax.ShapeDtypeStruct((nb, t, d), F32),
        compiler_params=_cparams("parallel", "parallel"),
        name="ffn_final" if final else "ffn",
    )(*args)


def _rope_partner(x, axis):
    n = x.shape[axis]
    idx = lax.broadcasted_iota(jnp.int32, x.shape, axis)
    first_half = (idx & 16) == 0
    return jnp.where(first_half, pltpu.roll(x, n - 16, axis), pltpu.roll(x, 16, axis))


def _inproj_kernel(s_ref, g_ref, sh_ref, sc_ref, wzk_ref, wqv_ref, ck_ref, sk_ref, cq_ref, sq_ref,
                   z_ref, k_ref, qT_ref, vT_ref, *, q_scale):
    ub = _adaln(s_ref[0], g_ref[...], sh_ref[0], sc_ref[0]).astype(BF16)
    tm = ub.shape[0]
    zk = _dot(ub, wzk_ref[...])
    hy = HYENA_N_PROJ * D_HYENA
    z_ref[0] = zk[:, :hy].astype(BF16)
    k = zk[:, hy:]
    reps = D_DIFF // LANES
    ck = jnp.concatenate([ck_ref[...]] * reps, axis=1)
    sk = jnp.concatenate([sk_ref[...]] * reps, axis=1)
    k_ref[0] = (k * ck + _rope_partner(k, 1) * sk).astype(BF16)
    qv = lax.dot_general(wqv_ref[...], ub, (((1,), (1,)), ((), ())), preferred_element_type=F32)
    q = qv[:D_DIFF]
    groups = D_DIFF // DIFF_HEAD_DIM
    cq = jnp.broadcast_to(cq_ref[...][None], (groups, DIFF_HEAD_DIM, tm)).reshape(D_DIFF, tm)
    sq = jnp.broadcast_to(sq_ref[...][None], (groups, DIFF_HEAD_DIM, tm)).reshape(D_DIFF, tm)
    q = (q * cq + _rope_partner(q, 0) * sq) * q_scale
    qT_ref[0] = q.astype(BF16)
    vT_ref[0, 0] = qv[D_DIFF:].astype(BF16)


def _input_projection(h, g, shift, scale, wzk, wqv, ck, sk, cq, sq):
    nb, n, d = h.shape
    tm = PROJ_TM
    hy = HYENA_N_PROJ * D_HYENA
    q_scale = DIFF_HEAD_DIM ** -0.5 * LOG2E
    args = [h, g.reshape(1, d), shift, scale, wzk, wqv, ck, sk, cq, sq]
    in_specs = [pl.BlockSpec((1, tm, d), lambda b, i: (b, i, 0)),
                _const_spec(args[1]), _mod_spec(shift, nb), _mod_spec(scale, nb),
                _const_spec(wzk), _const_spec(wqv),
                pl.BlockSpec((tm, LANES), lambda b, i: (i, 0)),
                pl.BlockSpec((tm, LANES), lambda b, i: (i, 0)),
                pl.BlockSpec((DIFF_HEAD_DIM, tm), lambda b, i: (0, i)),
                pl.BlockSpec((DIFF_HEAD_DIM, tm), lambda b, i: (0, i))]
    out_shape = (jax.ShapeDtypeStruct((nb, n, hy), BF16),
                 jax.ShapeDtypeStruct((nb, n, D_DIFF), BF16),
                 jax.ShapeDtypeStruct((nb, D_DIFF, n), BF16),
                 jax.ShapeDtypeStruct((nb, n // tm, D_DIFF, tm), BF16))
    out_specs = (pl.BlockSpec((1, tm, hy), lambda b, i: (b, i, 0)),
                 pl.BlockSpec((1, tm, D_DIFF), lambda b, i: (b, i, 0)),
                 pl.BlockSpec((1, D_DIFF, tm), lambda b, i: (b, 0, i)),
                 pl.BlockSpec((1, 1, D_DIFF, tm), lambda b, i: (b, i, 0, 0)))
    return pl.pallas_call(
        functools.partial(_inproj_kernel, q_scale=q_scale),
        grid=(nb, n // tm),
        in_specs=in_specs, out_specs=out_specs, out_shape=out_shape,
        compiler_params=_cparams("parallel", "parallel"),
        name="input_projection",
    )(*args)


def _inproj_ctx_kernel(s_ref, g_ref, sh_ref, sc_ref, wk_ref, wv_ref, k_ref, vT_ref):
    ub = _adaln(s_ref[0], g_ref[...], sh_ref[0], sc_ref[0]).astype(BF16)
    k_ref[0] = _dot(ub, wk_ref[...]).astype(BF16)
    vT_ref[0] = lax.dot_general(wv_ref[...], ub, (((1,), (1,)), ((), ())),
                                preferred_element_type=F32).astype(BF16)


def _context_projection(h, g, shift, scale, wk, wvT):
    nb, c, d = h.shape
    args = [h, g.reshape(1, d), shift, scale, wk, wvT]
    in_specs = [pl.BlockSpec((1, c, d), lambda b, i: (b, 0, 0)),
                _const_spec(args[1]), _mod_spec(shift, nb), _mod_spec(scale, nb),
                _const_spec(wk), _const_spec(wvT)]
    return pl.pallas_call(
        _inproj_ctx_kernel,
        grid=(nb, 1),
        in_specs=in_specs,
        out_specs=(pl.BlockSpec((1, c, D_DIFF), lambda b, i: (b, 0, 0)),
                   pl.BlockSpec((1, D_DIFF, c), lambda b, i: (b, 0, 0))),
        out_shape=(jax.ShapeDtypeStruct((nb, c, D_DIFF), BF16),
                   jax.ShapeDtypeStruct((nb, D_DIFF, c), BF16)),
        compiler_params=_cparams("parallel", "parallel"),
        name="context_projection",
    )(*args)


def _attn_kernel(qT_ref, k_ref, vT_ref, kc_ref, vTc_ref, lp_ref, sg_ref, o_ref,
                 acc0_ref, acc1_ref, *, n_chunks, tk):
    d = DIFF_HEAD_DIM
    q = qT_ref[0]
    tq = q.shape[1]
    row = lax.broadcasted_iota(jnp.int32, q.shape, 0)
    zero = jnp.zeros_like(q)
    qa = jnp.where(row < d, q, zero)
    qb = jnp.where(row < d, zero, q)
    acc0_ref[...] = jnp.zeros_like(acc0_ref)
    acc1_ref[...] = jnp.zeros_like(acc1_ref)

    def update(kc, vc, q_pad, acc_ref, m, l):
        s = _dot(kc, q_pad)
        m_new = jnp.maximum(m, jnp.max(s, axis=0, keepdims=True))
        p = jnp.exp2(s - m_new)
        alpha = jnp.exp2(m - m_new)
        l_new = alpha * l + jnp.sum(p, axis=0, keepdims=True)
        acc_ref[...] = alpha * acc_ref[...] + _dot(vc, p.astype(BF16))
        return m_new, l_new

    def body(j, carry):
        m0, l0, m1, l1 = carry
        kc = k_ref[0, pl.ds(pl.multiple_of(j * tk, tk), tk), :]
        vc = vT_ref[0, j]
        m0, l0 = update(kc, vc, qa, acc0_ref, m0, l0)
        m1, l1 = update(kc, vc, qb, acc1_ref, m1, l1)
        return m0, l0, m1, l1

    neg = jnp.full((1, tq), NEG_BIG, F32)
    zl = jnp.zeros((1, tq), F32)
    m0, l0, m1, l1 = lax.fori_loop(0, n_chunks, body, (neg, zl, neg, zl))
    m0, l0 = update(kc_ref[0], vTc_ref[0], qa, acc0_ref, m0, l0)
    m1, l1 = update(kc_ref[0], vTc_ref[0], qb, acc1_ref, m1, l1)

    lp = lp_ref[...]
    lam = (jnp.exp(jnp.sum(lp[0:1] * lp[1:2], axis=1, keepdims=True))
           - jnp.exp(jnp.sum(lp[2:3] * lp[3:4], axis=1, keepdims=True)) + LAM_INIT)
    oT = acc0_ref[...] / l0 - lam * (acc1_ref[...] / l1)
    ms = jnp.mean(oT * oT, axis=0, keepdims=True)
    oT = oT * lax.rsqrt(ms + SUBLN_EPS)
    o_ref[0] = ((oT.T * sg_ref[...]) * (1.0 - LAM_INIT)).astype(BF16)


def _diff_attention(qT, k, vT4, k_ctx, vT_ctx, lam_params, subln_g):
    nb, n, _ = k.shape
    hd = 2 * DIFF_HEAD_DIM
    n_chunks, tk = vT4.shape[1], vT4.shape[3]
    c = k_ctx.shape[1]
    tq = ATT_TQ
    in_specs = [pl.BlockSpec((1, hd, tq), lambda b, h, i: (b, h, i)),
                pl.BlockSpec((1, n, hd), lambda b, h, i: (b, 0, h)),
                pl.BlockSpec((1, n_chunks, hd, tk), lambda b, h, i: (b, 0, h, 0)),
                pl.BlockSpec((1, c, hd), lambda b, h, i: (b, 0, h)),
                pl.BlockSpec((1, hd, c), lambda b, h, i: (b, h, 0)),
                pl.BlockSpec(lam_params.shape, lambda b, h, i: (0, 0)),
                pl.BlockSpec((1, hd), lambda b, h, i: (0, 0))]
    return pl.pallas_call(
        functools.partial(_attn_kernel, n_chunks=n_chunks, tk=tk),
        grid=(nb, N_DIFF_HEADS, n // tq),
        in_specs=in_specs,
        out_specs=pl.BlockSpec((1, tq, hd), lambda b, h, i: (b, i, h)),
        out_shape=jax.ShapeDtypeStruct((nb, n, D_DIFF), BF16),
        scratch_shapes=[pltpu.VMEM((hd, tq), F32), pltpu.VMEM((hd, tq), F32)],
        compiler_params=_cparams("parallel", "parallel", "arbitrary"),
        name="diff_attention",
    )(qT, k, vT4, k_ctx, vT_ctx, lam_params, subln_g.reshape(1, hd))


def _hyena_prep_kernel(z_ref, zp_ref, zn_ref, w_ref, b_ref, u_ref, x0_ref):
    i = pl.program_id(1)
    last = pl.num_programs(1) - 1
    z = z_ref[0].astype(F32)
    tm = z.shape[0]
    prev = jnp.where(i == 0, 0.0, zp_ref[0, 7:8, :].astype(F32))
    nxt = jnp.where(i == last, 0.0, zn_ref[0, 0:1, :].astype(F32))
    row = lax.broadcasted_iota(jnp.int32, z.shape, 0)
    z_dn = jnp.where(row == 0, prev, pltpu.roll(z, 1, 0))
    z_up = jnp.where(row == tm - 1, nxt, pltpu.roll(z, tm - 1, 0))
    y = z_dn * w_ref[0:1] + z * w_ref[1:2] + z_up * w_ref[2:3] + b_ref[...]
    c = D_HYENA
    x0_ref[0] = y[:, :c].astype(BF16)
    u_ref[0] = (y[:, c:2 * c] * y[:, 2 * c:]).astype(BF16)


def _hyena_prep(z, conv_w, conv_b):
    nb, n, c3 = z.shape
    tm = PROJ_TM
    r = tm // 8
    nblk8 = n // 8
    in_specs = [pl.BlockSpec((1, tm, c3), lambda b, i: (b, i, 0)),
                pl.BlockSpec((1, 8, c3), lambda b, i: (b, jnp.maximum(i * r - 1, 0), 0)),
                pl.BlockSpec((1, 8, c3), lambda b, i: (b, jnp.minimum((i + 1) * r, nblk8 - 1), 0)),
                _const_spec(conv_w), pl.BlockSpec((1, c3), lambda b, i: (0, 0))]
    return pl.pallas_call(
        _hyena_prep_kernel,
        grid=(nb, n // tm),
        in_specs=in_specs,
        out_specs=(pl.BlockSpec((1, tm, D_HYENA), lambda b, i: (b, i, 0)),
                   pl.BlockSpec((1, tm, D_HYENA), lambda b, i: (b, i, 0))),
        out_shape=(jax.ShapeDtypeStruct((nb, n, D_HYENA), BF16),
                   jax.ShapeDtypeStruct((nb, n, D_HYENA), BF16)),
        compiler_params=_cparams("parallel", "parallel"),
        name="hyena_prep",
    )(z, z, z, conv_w, conv_b.reshape(1, c3))


def _filter_kernel(t_ref, om_ref, band_ref, w1_ref, b1_ref, wi_ref, bi_ref, fr_ref, wo_ref, dl_ref, h_ref):
    t = t_ref[...]
    lane = lax.broadcasted_iota(jnp.int32, (t.shape[0], LANES), 1)
    ang = om_ref[...] * band_ref[...]
    nb = FILTER_EMB_BANDS
    emb = jnp.where(lane == 0, t,
                    jnp.where(lane <= nb, jnp.cos(ang),
                              jnp.where(lane <= 2 * nb, -jnp.sin(ang), 0.0)))
    fr = fr_ref[...]
    h = jnp.sin(fr * (_dot3(emb, w1_ref[...]) + b1_ref[...]))
    for i in range(FILTER_INNER):
        h = jnp.sin(fr * (_dot3(h, wi_ref[i]) + bi_ref[i]))
    h = _dot3(h, wo_ref[...]) * jnp.exp(-t * dl_ref[...])
    c = D_HYENA
    h_ref[0] = h[:, :c]
    first = (pl.program_id(0) == 0) & (lax.broadcasted_iota(jnp.int32, (t.shape[0], c), 0) == 0)
    h_ref[1] = jnp.where(first, 0.0, h[:, c:])


def _hyena_filter(n, w1, b1, w_inner, b_inner, freq, w_out):
    tl = min(FILT_TL, n)
    t = jnp.linspace(0.0, 1.0, n, dtype=F32)[:, None]
    omega = (2.0 * math.pi / n) * jnp.arange(n, dtype=F32)[:, None]
    bands = jnp.linspace(1e-4, FILTER_EMB_BANDS - 1, FILTER_EMB_BANDS, dtype=F32)
    pad = LANES - 1 - 2 * FILTER_EMB_BANDS
    band_row = jnp.concatenate([jnp.zeros((1,), F32), bands, bands, jnp.zeros((pad,), F32)])[None, :]
    w1p = jnp.concatenate([w1.astype(F32), jnp.zeros((LANES - w1.shape[0], w1.shape[1]), F32)], axis=0)
    min_decay = math.log(DECAY_TARGET) / SLOW_DECAY_PCT
    max_decay = math.log(DECAY_TARGET) / FAST_DECAY_PCT
    deltas = jnp.abs(jnp.linspace(min_decay, max_decay, D_HYENA, dtype=F32))
    dl = jnp.concatenate([deltas, deltas])[None, :]
    fh = FILTER_HIDDEN
    args = [t, omega, band_row, w1p, b1.reshape(1, fh), w_inner, b_inner.reshape(FILTER_INNER, 1, fh),
            freq.reshape(1, fh), w_out, dl]

    def cs(a):
        nd = a.ndim
        return pl.BlockSpec(a.shape, lambda i: (0,) * nd)

    in_specs = [pl.BlockSpec((tl, 1), lambda i: (i, 0)), pl.BlockSpec((tl, 1), lambda i: (i, 0))]
    in_specs += [cs(a) for a in args[2:]]
    return pl.pallas_call(
        _filter_kernel,
        grid=(n // tl,),
        in_specs=in_specs,
        out_specs=pl.BlockSpec((2, tl, D_HYENA), lambda i: (0, i, 0)),
        out_shape=jax.ShapeDtypeStruct((2, n, D_HYENA), F32),
        compiler_params=_cparams("parallel"),
        name="hyena_filter",
    )(*args)


def _dft_tables(n):
    m2 = FFT_M2
    m1 = 2 * n // m2
    m = 2 * n
    k1 = np.arange(m1, dtype=np.float64)[:, None]
    n1 = np.arange(m1 // 2, dtype=np.float64)[None, :]
    ang1 = 2.0 * np.pi * k1 * n1 / m1
    f1 = np.concatenate([np.cos(ang1), -np.sin(ang1)], axis=0)
    finv = np.concatenate([np.cos(ang1).T, -np.sin(ang1).T], axis=1)
    a = np.arange(m2, dtype=np.float64)
    ang2 = 2.0 * np.pi * a[:, None] * a[None, :] / m2
    angt = 2.0 * np.pi * k1 * a[None, :] / m
    return dict(
        m1=m1,
        f1=jnp.asarray(f1, F32), finv=jnp.asarray(finv, F32),
        f2r=jnp.asarray(np.cos(ang2), F32), f2i=jnp.asarray(-np.sin(ang2), F32),
        twr=jnp.asarray(np.cos(angt)[:, None, :], F32), twi=jnp.asarray(-np.sin(angt)[:, None, :], F32))


def _dft_a_kernel(x_ref, f_ref, o_ref):
    o_ref[0] = _dot(f_ref[...].astype(BF16), x_ref[0].astype(BF16)).astype(BF16)


def _dft_stage_a(x, f1):
    g, r, w = x.shape
    tn = min(FFT_TN, w)
    rows = f1.shape[0]
    return pl.pallas_call(
        _dft_a_kernel,
        grid=(g, w // tn),
        in_specs=[pl.BlockSpec((1, r, tn), lambda b, j: (b, 0, j)),
                  pl.BlockSpec(f1.shape, lambda b, j: (0, 0))],
        out_specs=pl.BlockSpec((1, rows, tn), lambda b, j: (b, 0, j)),
        out_shape=jax.ShapeDtypeStruct((g, rows, w), BF16),
        compiler_params=_cparams("parallel", "parallel"),
        name="dft_stage_a",
    )(x, f1)


def _twiddled_dft(f2r, f2i, twr, twi):
    gr = f2r * twr - f2i * twi
    gi = f2r * twi + f2i * twr
    return jnp.concatenate([jnp.concatenate([gr, -gi], axis=1),
                            jnp.concatenate([gi, gr], axis=1)], axis=0)


def _filter_spectrum_kernel(a_ref, f2r_ref, f2i_ref, twr_ref, twi_ref, k_ref, *, kc):
    m2 = FFT_M2
    for j in range(kc):
        g = _twiddled_dft(f2r_ref[...], f2i_ref[...], twr_ref[j], twi_ref[j]).astype(BF16)
        xf = _dot(g, a_ref[0, :, j].reshape(2 * m2, -1))
        xb = _dot(g, a_ref[1, :, j].reshape(2 * m2, -1))
        k_ref[0, j] = xf[:m2] + xb[:m2]
        k_ref[1, j] = xf[m2:] - xb[m2:]


def _filter_spectrum(a5, tabs):
    _, _, m1, m2, c = a5.shape
    kc = FFT_KC
    in_specs = [pl.BlockSpec((2, 2, kc, m2, c), lambda i: (0, 0, i, 0, 0)),
                pl.BlockSpec((m2, m2), lambda i: (0, 0)), pl.BlockSpec((m2, m2), lambda i: (0, 0)),
                pl.BlockSpec((kc, 1, m2), lambda i: (i, 0, 0)), pl.BlockSpec((kc, 1, m2), lambda i: (i, 0, 0))]
    return pl.pallas_call(
        functools.partial(_filter_spectrum_kernel, kc=kc),
        grid=(m1 // kc,),
        in_specs=in_specs,
        out_specs=pl.BlockSpec((2, kc, m2, c), lambda i: (0, i, 0, 0)),
        out_shape=jax.ShapeDtypeStruct((2, m1, m2, c), F32),
        compiler_params=_cparams("parallel"),
        name="filter_spectrum",
    )(a5, tabs["f2r"], tabs["f2i"], tabs["twr"], tabs["twi"])


def _spectral_kernel(a_ref, kf_ref, f2r_ref, f2i_ref, twr_ref, twi_ref, b_ref, *, kc, nb):
    m2 = FFT_M2
    for j in range(kc):
        g32 = _twiddled_dft(f2r_ref[...], f2i_ref[...], twr_ref[j], twi_ref[j])
        g = g32.astype(BF16)
        gt = g32.T.astype(BF16)
        kr = kf_ref[0, j]
        ki = kf_ref[1, j]
        for b in range(nb):
            x = _dot(g, a_ref[b, :, j].reshape(2 * m2, -1))
            xr, xi = x[:m2], x[m2:]
            y = jnp.concatenate([xr * kr - xi * ki, xr * ki + xi * kr], axis=0).astype(BF16)
            z = _dot(gt, y)
            b_ref[b, :, j] = z.reshape(2, m2, -1).astype(BF16)


def _spectral_multiply(a5, kf, tabs):
    nb, _, m1, m2, c = a5.shape
    kc = FFT_KC
    in_specs = [pl.BlockSpec((nb, 2, kc, m2, c), lambda i: (0, 0, i, 0, 0)),
                pl.BlockSpec((2, kc, m2, c), lambda i: (0, i, 0, 0)),
                pl.BlockSpec((m2, m2), lambda i: (0, 0)), pl.BlockSpec((m2, m2), lambda i: (0, 0)),
                pl.BlockSpec((kc, 1, m2), lambda i: (i, 0, 0)), pl.BlockSpec((kc, 1, m2), lambda i: (i, 0, 0))]
    return pl.pallas_call(
        functools.partial(_spectral_kernel, kc=kc, nb=nb),
        grid=(m1 // kc,),
        in_specs=in_specs,
        out_specs=pl.BlockSpec((nb, 2, kc, m2, c), lambda i: (0, 0, i, 0, 0)),
        out_shape=jax.ShapeDtypeStruct((nb, 2, m1, m2, c), BF16),
        compiler_params=_cparams("parallel"),
        name="spectral_multiply",
    )(a5, kf, tabs["f2r"], tabs["f2i"], tabs["twr"], tabs["twi"])


def _idft_a_kernel(b_ref, f_ref, x0_ref, u_ref, bias_ref, y_ref, *, inv_m):
    conv = _dot(f_ref[...].astype(BF16), b_ref[0]) * inv_m
    u = u_ref[0].astype(F32)
    y_ref[0] = (x0_ref[0].astype(F32) * (conv + u * bias_ref[...])).astype(BF16)


def _idft_stage_a(bm, finv, x0v, uv, bias_row, inv_m):
    g, rows, w = bm.shape
    r = finv.shape[0]
    tn = min(FFT_TN, w)
    return pl.pallas_call(
        functools.partial(_idft_a_kernel, inv_m=inv_m),
        grid=(g, w // tn),
        in_specs=[pl.BlockSpec((1, rows, tn), lambda b, j: (b, 0, j)),
                  pl.BlockSpec(finv.shape, lambda b, j: (0, 0)),
                  pl.BlockSpec((1, r, tn), lambda b, j: (b, 0, j)),
                  pl.BlockSpec((1, r, tn), lambda b, j: (b, 0, j)),
                  pl.BlockSpec((1, tn), lambda b, j: (0, 0))],
        out_specs=pl.BlockSpec((1, r, tn), lambda b, j: (b, 0, j)),
        out_shape=jax.ShapeDtypeStruct((g, r, w), BF16),
        compiler_params=_cparams("parallel", "parallel"),
        name="idft_stage_a",
    )(bm, finv, x0v, uv, bias_row)


def _hyena_mixer(z, conv_w, conv_b, w1, b1, w_inner, b_inner, freq, w_out, bias):
    nb, n, _ = z.shape
    c = D_HYENA
    tabs = _dft_tables(n)
    m1, m2 = tabs["m1"], FFT_M2
    w = m2 * c
    u, x0 = _hyena_prep(z, conv_w, conv_b)
    h = _hyena_filter(n, w1, b1, w_inner, b_inner, freq, w_out)
    ah = _dft_stage_a(h.reshape(2, m1 // 2, w), tabs["f1"])
    kf = _filter_spectrum(ah.reshape(2, 2, m1, m2, c), tabs)
    au = _dft_stage_a(u.reshape(nb, m1 // 2, w), tabs["f1"])
    bm = _spectral_multiply(au.reshape(nb, 2, m1, m2, c), kf, tabs)
    tn = min(FFT_TN, w)
    bias_row = jnp.tile(bias.astype(F32), tn // c)[None, :]
    y = _idft_stage_a(bm.reshape(nb, 2 * m1, w), tabs["finv"], x0.reshape(nb, m1 // 2, w),
                      u.reshape(nb, m1 // 2, w), bias_row, 1.0 / (2 * n))
    return y.reshape(nb, n, c)


def _outproj_kernel(h_ref, yh_ref, yd_ref, wa_ref, wb_ref, gt_ref, o_ref):
    mix = _dot(yh_ref[0], wa_ref[...]) + _dot(yd_ref[0], wb_ref[...])
    o_ref[0] = h_ref[0] + gt_ref[0] * mix


def _output_projection(h, y_hy, y_diff, wo_a, wo_b, gate):
    nb, n, d = h.shape
    tm = PROJ_TM
    in_specs = [pl.BlockSpec((1, tm, d), lambda b, i: (b, i, 0)),
                pl.BlockSpec((1, tm, D_HYENA), lambda b, i: (b, i, 0)),
                pl.BlockSpec((1, tm, D_DIFF), lambda b, i: (b, i, 0)),
                _const_spec(wo_a), _const_spec(wo_b), _mod_spec(gate, nb)]
    return pl.pallas_call(
        _outproj_kernel,
        grid=(nb, n // tm),
        in_specs=in_specs,
        out_specs=pl.BlockSpec((1, tm, d), lambda b, i: (b, i, 0)),
        out_shape=jax.ShapeDtypeStruct((nb, n, d), F32),
        compiler_params=_cparams("parallel", "parallel"),
        name="output_projection",
    )(h, y_hy, y_diff, wo_a, wo_b, gate)


def _rope_tables(n):
    rows = n // GRID_W
    row = jnp.broadcast_to(jnp.arange(rows, dtype=F32)[:, None], (rows, GRID_W)).reshape(-1)
    col = jnp.broadcast_to(jnp.arange(GRID_W, dtype=F32)[None, :], (rows, GRID_W)).reshape(-1)
    axis_dim = DIFF_HEAD_DIM // 2
    inv_freq = ROPE_BASE ** (-jnp.arange(0, axis_dim, 2, dtype=F32) / axis_dim)
    ang_r = row[:, None] * inv_freq
    ang_c = col[:, None] * inv_freq
    cos64 = jnp.concatenate([jnp.cos(ang_r), jnp.cos(ang_r), jnp.cos(ang_c), jnp.cos(ang_c)], axis=1)
    sin64 = jnp.concatenate([-jnp.sin(ang_r), jnp.sin(ang_r), -jnp.sin(ang_c), jnp.sin(ang_c)], axis=1)
    reps = LANES // DIFF_HEAD_DIM
    return (jnp.tile(cos64, (1, reps)), jnp.tile(sin64, (1, reps)), cos64.T, sin64.T)


def _chunk_cols(w, tf):
    d, f = w.shape
    return w.reshape(d, f // tf, tf).transpose(1, 0, 2).astype(BF16)


def kernel(x, c, ctx, c_ctx, ada_w, ada_b, norm_g, ffn_w_gate, ffn_w_up, ffn_w_down, w_in, w_out,
           hyena_conv_w, hyena_conv_b, filt_w1, filt_b1, filt_w_inner, filt_b_inner, filt_sin_freq,
           filt_w_out, hyena_bias, diff_lambda, diff_subln_g, final_g):
    nb, n, d = x.shape
    layer = 0
    hy_end = HYENA_N_PROJ * D_HYENA
    q_end = hy_end + D_DIFF
    k_end = q_end + D_DIFF

    cc = jnp.concatenate([c, c_ctx[None, :], jnp.zeros((8 - nb - 1, d), F32)], axis=0)
    mod = _modulation(cc, ada_w[layer], ada_b[layer])
    mod_lat = [mod[:nb, j * d:(j + 1) * d][:, None, :] for j in range(N_ADA)]
    mod_ctx = [mod[nb:nb + 1, j * d:(j + 1) * d][:, None, :] for j in range(N_ADA)]

    def ffn_weights(i):
        return (_chunk_cols(ffn_w_gate[layer, i], FFN_TF), _chunk_cols(ffn_w_up[layer, i], FFN_TF),
                ffn_w_down[layer, i].reshape(FFN_HIDDEN // FFN_TF, FFN_TF, d).astype(BF16))

    ffn0 = ffn_weights(0)
    h_lat = _ffn_half_step(x, norm_g[layer, 0], mod_lat[0], mod_lat[1], mod_lat[2], *ffn0)
    h_ctx = _ffn_half_step(ctx, norm_g[layer, 0], mod_ctx[0], mod_ctx[1], mod_ctx[2], *ffn0)

    w = w_in[layer]
    wzk = jnp.concatenate([w[:, :hy_end], w[:, q_end:k_end]], axis=1).astype(BF16)
    wqv = jnp.concatenate([w[:, hy_end:q_end], w[:, k_end:]], axis=1).T.astype(BF16)
    ck, sk, cq, sq = _rope_tables(n)
    z, k_lat, qT, vT4 = _input_projection(h_lat, norm_g[layer, 1], mod_lat[3], mod_lat[4],
                                          wzk, wqv, ck, sk, cq, sq)
    k_ctx, vT_ctx = _context_projection(h_ctx, norm_g[layer, 1], mod_ctx[3], mod_ctx[4],
                                        w[:, q_end:k_end].astype(BF16), w[:, k_end:].T.astype(BF16))

    y_diff = _diff_attention(qT, k_lat, vT4, k_ctx, vT_ctx, diff_lambda[layer].astype(F32),
                             diff_subln_g[layer])
    y_hy = _hyena_mixer(z, hyena_conv_w[layer], hyena_conv_b[layer], filt_w1[layer], filt_b1[layer],
                        filt_w_inner[layer], filt_b_inner[layer], filt_sin_freq[layer],
                        filt_w_out[layer], hyena_bias[layer])

    wo = w_out[layer].astype(BF16)
    h_lat = _output_projection(h_lat, y_hy, y_diff, wo[:D_HYENA], wo[D_HYENA:], mod_lat[5])
    return _ffn_half_step(h_lat, norm_g[layer, 2], mod_lat[6], mod_lat[7], mod_lat[8],
                          *ffn_weights(1), final_g=final_g)
```

```python
import functools
import math

import numpy as np
import jax
import jax.numpy as jnp
from jax import lax
from jax.experimental import pallas as pl
from jax.experimental.pallas import tpu as pltpu

F32 = jnp.float32
BF16 = jnp.bfloat16

N_ADA = 9
FFN_HIDDEN = 2816
D_HYENA = 512
HYENA_N_PROJ = 3
FILTER_EMB_BANDS = 16
FILTER_HIDDEN = 64
FILTER_INNER = 2
DECAY_TARGET = 1e-2
FAST_DECAY_PCT = 0.3
SLOW_DECAY_PCT = 1.5
N_DIFF_HEADS = 4
DIFF_HEAD_DIM = 64
D_DIFF = N_DIFF_HEADS * 2 * DIFF_HEAD_DIM
GRID_W = 64
ROPE_BASE = 10000.0
RMS_EPS = 1e-6
SUBLN_EPS = 1e-5
LAM_INIT = 0.8 - 0.6 * math.exp(-0.3 * 0)

LANES = 128
BF16_ROWS = 16
V7X_VMEM_BYTES = 64 * 2**20
VMEM_LIMIT = 56 * 2**20

FFN_TM = 512
FFN_TF = 256
PROJ_TM = 512
ATT_TQ = 1024
FFT_M2 = 128
FFT_TN = 4096
FFT_KC = 8
FILT_TL = 1024
NEG_BIG = -1e30

LOG2E = 1.4426950408889634


def _cparams(*sem):
    return pltpu.CompilerParams(dimension_semantics=sem, vmem_limit_bytes=VMEM_LIMIT)


def _dot(a, b):
    return jnp.dot(a, b, preferred_element_type=F32)


def _dot3(a, b):
    a_hi = a.astype(BF16)
    a_lo = (a - a_hi.astype(F32)).astype(BF16)
    b_hi = b.astype(BF16)
    b_lo = (b - b_hi.astype(F32)).astype(BF16)
    return _dot(a_hi, b_hi) + (_dot(a_lo, b_hi) + _dot(a_hi, b_lo))


def _silu(x):
    return x * (1.0 / (1.0 + jnp.exp(-x)))


def _adaln(s, g, shift, scale):
    ms = jnp.mean(s * s, axis=-1, keepdims=True)
    return (s * lax.rsqrt(ms + RMS_EPS) * g) * (1.0 + scale) + shift


def _mod_kernel(c_ref, w_ref, b_ref, o_ref):
    o_ref[...] = _dot3(_silu(c_ref[...]), w_ref[...]) + b_ref[...]


def _modulation(cc, ada_w, ada_b):
    rows, d = cc.shape
    n = ada_w.shape[1]
    tn = 768
    return pl.pallas_call(
        _mod_kernel,
        grid=(n // tn,),
        in_specs=[pl.BlockSpec((rows, d), lambda j: (0, 0)),
                  pl.BlockSpec((d, tn), lambda j: (0, j)),
                  pl.BlockSpec((1, tn), lambda j: (0, j))],
        out_specs=pl.BlockSpec((rows, tn), lambda j: (0, j)),
        out_shape=jax.ShapeDtypeStruct((rows, n), F32),
        compiler_params=_cparams("parallel"),
        name="modulation",
    )(cc, ada_w, ada_b.reshape(1, n))


def _mod_spec(arr, nb):
    d = arr.shape[-1]
    if arr.shape[0] == nb:
        return pl.BlockSpec((1, 1, d), lambda b, i: (b, 0, 0))
    return pl.BlockSpec((1, 1, d), lambda b, i: (0, 0, 0))


def _const_spec(arr):
    nd = arr.ndim
    return pl.BlockSpec(arr.shape, lambda b, i: (0,) * nd)


def _ffn_kernel(s_ref, g_ref, sh_ref, sc_ref, gt_ref, wg_ref, wu_ref, wd_ref, *rest, n_chunks, final):
    if final:
        fg_ref, o_ref = rest
    else:
        (o_ref,) = rest
    s = s_ref[0]
    ub = _adaln(s, g_ref[...], sh_ref[0], sc_ref[0]).astype(BF16)
    acc = jnp.zeros(s.shape, F32)
    for f in range(n_chunks):
        gate = _dot(ub, wg_ref[f])
        up = _dot(ub, wu_ref[f])
        a = (_silu(gate) * up).astype(BF16)
        acc = acc + _dot(a, wd_ref[f])
    out = s + (0.5 * gt_ref[0]) * acc
    if final:
        ms = jnp.mean(out * out, axis=-1, keepdims=True)
        out = out * lax.rsqrt(ms + RMS_EPS) * fg_ref[...]
    o_ref[0] = out


def _ffn_half_step(s, g, shift, scale, gate, wg3, wu3, wd3, final_g=None):
    nb, t, d = s.shape
    tm = min(FFN_TM, t)
    n_chunks = wg3.shape[0]
    final = final_g is not None
    args = [s, g.reshape(1, d), shift, scale, gate, wg3, wu3, wd3]
    in_specs = [pl.BlockSpec((1, tm, d), lambda b, i: (b, i, 0)),
                _const_spec(args[1]),
                _mod_spec(shift, nb), _mod_spec(scale, nb), _mod_spec(gate, nb),
                _const_spec(wg3), _const_spec(wu3), _const_spec(wd3)]
    if final:
        args.append(final_g.reshape(1, d))
        in_specs.append(_const_spec(args[-1]))
    return pl.pallas_call(
        functools.partial(_ffn_kernel, n_chunks=n_chunks, final=final),
        grid=(nb, t // tm),
        in_specs=in_specs,
        out_specs=pl.BlockSpec((1, tm, d), lambda b, i: (b, i, 0)),
        out_shape=jax.ShapeDtypeStruct((nb, t, d), F32),
        compiler_params=_cparams("parallel", "parallel"),
        name="ffn_final" if final else "ffn",
    )(*args)


def _rope_partner(x, axis):
    n = x.shape[axis]
    idx = lax.broadcasted_iota(jnp.int32, x.shape, axis)
    first_half = (idx & 16) == 0
    return jnp.where(first_half, pltpu.roll(x, n - 16, axis), pltpu.roll(x, 16, axis))


def _inproj_kernel(s_ref, g_ref, sh_ref, sc_ref, wzk_ref, wqv_ref, ck_ref, sk_ref, cq_ref, sq_ref,
                   z_ref, k_ref, qT_ref, vT_ref, *, q_scale):
    ub = _adaln(s_ref[0], g_ref[...], sh_ref[0], sc_ref[0]).astype(BF16)
    tm = ub.shape[0]
    zk = _dot(ub, wzk_ref[...])
    hy = HYENA_N_PROJ * D_HYENA
    z_ref[0] = zk[:, :hy].astype(BF16)
    k = zk[:, hy:]
    reps = D_DIFF // LANES
    ck = jnp.concatenate([ck_ref[...]] * reps, axis=1)
    sk = jnp.concatenate([sk_ref[...]] * reps, axis=1)
    k_ref[0] = (k * ck + _rope_partner(k, 1) * sk).astype(BF16)
    qv = lax.dot_general(wqv_ref[...], ub, (((1,), (1,)), ((), ())), preferred_element_type=F32)
    q = qv[:D_DIFF]
    groups = D_DIFF // DIFF_HEAD_DIM
    cq = jnp.broadcast_to(cq_ref[...][None], (groups, DIFF_HEAD_DIM, tm)).reshape(D_DIFF, tm)
    sq = jnp.broadcast_to(sq_ref[...][None], (groups, DIFF_HEAD_DIM, tm)).reshape(D_DIFF, tm)
    q = (q * cq + _rope_partner(q, 0) * sq) * q_scale
    qT_ref[0] = q.astype(BF16)
    vT_ref[0, 0] = qv[D_DIFF:].astype(BF16)


def _input_projection(h, g, shift, scale, wzk, wqv, ck, sk, cq, sq):
    nb, n, d = h.shape
    tm = PROJ_TM
    hy = HYENA_N_PROJ * D_HYENA
    q_scale = DIFF_HEAD_DIM ** -0.5 * LOG2E
    args = [h, g.reshape(1, d), shift, scale, wzk, wqv, ck, sk, cq, sq]
    in_specs = [pl.BlockSpec((1, tm, d), lambda b, i: (b, i, 0)),
                _const_spec(args[1]), _mod_spec(shift, nb), _mod_spec(scale, nb),
                _const_spec(wzk), _const_spec(wqv),
                pl.BlockSpec((tm, LANES), lambda b, i: (i, 0)),
                pl.BlockSpec((tm, LANES), lambda b, i: (i, 0)),
                pl.BlockSpec((DIFF_HEAD_DIM, tm), lambda b, i: (0, i)),
                pl.BlockSpec((DIFF_HEAD_DIM, tm), lambda b, i: (0, i))]
    out_shape = (jax.ShapeDtypeStruct((nb, n, hy), BF16),
                 jax.ShapeDtypeStruct((nb, n, D_DIFF), BF16),
                 jax.ShapeDtypeStruct((nb, D_DIFF, n), BF16),
                 jax.ShapeDtypeStruct((nb, n // tm, D_DIFF, tm), BF16))
    out_specs = (pl.BlockSpec((1, tm, hy), lambda b, i: (b, i, 0)),
                 pl.BlockSpec((1, tm, D_DIFF), lambda b, i: (b, i, 0)),
                 pl.BlockSpec((1, D_DIFF, tm), lambda b, i: (b, 0, i)),
                 pl.BlockSpec((1, 1, D_DIFF, tm), lambda b, i: (b, i, 0, 0)))
    return pl.pallas_call(
        functools.partial(_inproj_kernel, q_scale=q_scale),
        grid=(nb, n // tm),
        in_specs=in_specs, out_specs=out_specs, out_shape=out_shape,
        compiler_params=_cparams("parallel", "parallel"),
        name="input_projection",
    )(*args)


def _inproj_ctx_kernel(s_ref, g_ref, sh_ref, sc_ref, wk_ref, wv_ref, k_ref, vT_ref):
    ub = _adaln(s_ref[0], g_ref[...], sh_ref[0], sc_ref[0]).astype(BF16)
    k_ref[0] = _dot(ub, wk_ref[...]).astype(BF16)
    vT_ref[0] = lax.dot_general(wv_ref[...], ub, (((1,), (1,)), ((), ())),
                                preferred_element_type=F32).astype(BF16)


def _context_projection(h, g, shift, scale, wk, wvT):
    nb, c, d = h.shape
    args = [h, g.reshape(1, d), shift, scale, wk, wvT]
    in_specs = [pl.BlockSpec((1, c, d), lambda b, i: (b, 0, 0)),
                _const_spec(args[1]), _mod_spec(shift, nb), _mod_spec(scale, nb),
                _const_spec(wk), _const_spec(wvT)]
    return pl.pallas_call(
        _inproj_ctx_kernel,
        grid=(nb, 1),
        in_specs=in_specs,
        out_specs=(pl.BlockSpec((1, c, D_DIFF), lambda b, i: (b, 0, 0)),
                   pl.BlockSpec((1, D_DIFF, c), lambda b, i: (b, 0, 0))),
        out_shape=(jax.ShapeDtypeStruct((nb, c, D_DIFF), BF16),
                   jax.ShapeDtypeStruct((nb, D_DIFF, c), BF16)),
        compiler_params=_cparams("parallel", "parallel"),
        name="context_projection",
    )(*args)


def _attn_kernel(qT_ref, k_ref, vT_ref, kc_ref, vTc_ref, lp_ref, sg_ref, o_ref,
                 acc0_ref, acc1_ref, qa_ref, qb_ref, *sp_refs, n_chunks, tk):
    d = DIFF_HEAD_DIM
    s_refs = (sp_refs[0:2], sp_refs[2:4])
    p_refs = (sp_refs[4:6], sp_refs[6:8])
    q = qT_ref[0]
    tq = q.shape[1]
    row = lax.broadcasted_iota(jnp.int32, q.shape, 0)
    zero = jnp.zeros_like(q)
    qa_ref[...] = jnp.where(row < d, q, zero)
    qb_ref[...] = jnp.where(row < d, zero, q)
    acc0_ref[...] = jnp.zeros_like(acc0_ref)
    acc1_ref[...] = jnp.zeros_like(acc1_ref)
    q_refs = (qa_ref, qb_ref)
    acc_refs = (acc0_ref, acc1_ref)

    def with_ones(vc):
        r = lax.broadcasted_iota(jnp.int32, (BF16_ROWS, vc.shape[1]), 0)
        return jnp.concatenate([vc, jnp.where(r == 0, 1.0, 0.0).astype(BF16)], axis=0)

    def scores(kc, slot):
        cmax = []
        for h in range(2):
            s = _dot(kc, q_refs[h][...])
            s_refs[slot][h][...] = s
            cmax.append(jnp.max(s, axis=0, keepdims=True))
        return tuple(cmax)

    def probs(s, cmax, m):
        m_new = jnp.maximum(m, cmax)
        return jnp.exp2((s - m_new).astype(BF16)), m_new, jnp.exp2(m - m_new)

    def softmax_stage(slot, cmax, stats):
        out = []
        for h in range(2):
            p, m, alpha = probs(s_refs[slot][h][...], cmax[h], stats[h][0])
            p_refs[slot][h][...] = p
            out.append((m, alpha))
        return tuple(out)

    def values(vc, slot, stats):
        for h in range(2):
            acc_refs[h][...] = stats[h][1] * acc_refs[h][...] + _dot(vc, p_refs[slot][h][...])

    def lat_keys(t):
        return k_ref[0, pl.ds(pl.multiple_of(t * tk, tk), tk), :]

    def step(t, slot, cmax_other, stats, with_scores=True):
        cmax = None
        if with_scores and slot == 1:
            cmax = scores(lat_keys(t + 2), slot)
        new_stats = softmax_stage(1 - slot, cmax_other, stats)
        values(with_ones(vT_ref[0, t]), slot, stats)
        if with_scores and slot == 0:
            cmax = scores(lat_keys(t + 2), slot)
        return cmax, new_stats

    neg = jnp.full((1, tq), NEG_BIG, F32)
    one = jnp.ones((1, tq), F32)
    cmax0 = scores(lat_keys(0), 0)
    cmax1 = scores(lat_keys(1), 1)
    stats = softmax_stage(0, cmax0, ((neg, one), (neg, one)))

    def body(i, carry):
        cmax1, stats = carry
        cmax0, stats = step(2 * i, 0, cmax1, stats)
        return step(2 * i + 1, 1, cmax0, stats)

    cmax1, stats = lax.fori_loop(0, (n_chunks - 2) // 2, body, (cmax1, stats))
    _, stats = step(n_chunks - 2, 0, cmax1, stats, with_scores=False)
    values(with_ones(vT_ref[0, n_chunks - 1]), 1, stats)

    kc = kc_ref[0]
    vc = with_ones(vTc_ref[0])
    for h in range(2):
        s = _dot(kc, q_refs[h][...])
        p, _, alpha = probs(s, jnp.max(s, axis=0, keepdims=True), stats[h][0])
        acc_refs[h][...] = alpha * acc_refs[h][...] + _dot(vc, p)

    lp = lp_ref[...]
    lam = (jnp.exp(jnp.sum(lp[0:1] * lp[1:2], axis=1, keepdims=True))
           - jnp.exp(jnp.sum(lp[2:3] * lp[3:4], axis=1, keepdims=True)) + LAM_INIT)
    hd = 2 * d
    o0 = acc0_ref[:hd, :] / acc0_ref[hd:hd + 1, :]
    o1 = acc1_ref[:hd, :] / acc1_ref[hd:hd + 1, :]
    oT = o0 - lam * o1
    ms = jnp.mean(oT * oT, axis=0, keepdims=True)
    oT = oT * lax.rsqrt(ms + SUBLN_EPS)
    o_ref[0] = ((oT.T * sg_ref[...]) * (1.0 - LAM_INIT)).astype(BF16)


def _diff_attention(qT, k, vT4, k_ctx, vT_ctx, lam_params, subln_g):
    nb, n, _ = k.shape
    hd = 2 * DIFF_HEAD_DIM
    n_chunks, tk = vT4.shape[1], vT4.shape[3]
    c = k_ctx.shape[1]
    tq = ATT_TQ
    in_specs = [pl.BlockSpec((1, hd, tq), lambda b, h, i: (b, h, i)),
                pl.BlockSpec((1, n, hd), lambda b, h, i: (b, 0, h)),
                pl.BlockSpec((1, n_chunks, hd, tk), lambda b, h, i: (b, 0, h, 0)),
                pl.BlockSpec((1, c, hd), lambda b, h, i: (b, 0, h)),
                pl.BlockSpec((1, hd, c), lambda b, h, i: (b, h, 0)),
                pl.BlockSpec(lam_params.shape, lambda b, h, i: (0, 0)),
                pl.BlockSpec((1, hd), lambda b, h, i: (0, 0))]
    return pl.pallas_call(
        functools.partial(_attn_kernel, n_chunks=n_chunks, tk=tk),
        grid=(nb, N_DIFF_HEADS, n // tq),
        in_specs=in_specs,
        out_specs=pl.BlockSpec((1, tq, hd), lambda b, h, i: (b, i, h)),
        out_shape=jax.ShapeDtypeStruct((nb, n, D_DIFF), BF16),
        scratch_shapes=([pltpu.VMEM((hd + BF16_ROWS, tq), F32)] * 2 + [pltpu.VMEM((hd, tq), BF16)] * 2
                        + [pltpu.VMEM((tk, tq), F32)] * 4 + [pltpu.VMEM((tk, tq), BF16)] * 4),
        compiler_params=_cparams("parallel", "parallel", "arbitrary"),
        name="diff_attention",
    )(qT, k, vT4, k_ctx, vT_ctx, lam_params, subln_g.reshape(1, hd))


def _hyena_prep_kernel(z_ref, zp_ref, zn_ref, w_ref, b_ref, u_ref, x0_ref):
    i = pl.program_id(1)
    last = pl.num_programs(1) - 1
    z = z_ref[0].astype(F32)
    tm = z.shape[0]
    prev = jnp.where(i == 0, 0.0, zp_ref[0, 7:8, :].astype(F32))
    nxt = jnp.where(i == last, 0.0, zn_ref[0, 0:1, :].astype(F32))
    row = lax.broadcasted_iota(jnp.int32, z.shape, 0)
    z_dn = jnp.where(row == 0, prev, pltpu.roll(z, 1, 0))
    z_up = jnp.where(row == tm - 1, nxt, pltpu.roll(z, tm - 1, 0))
    y = z_dn * w_ref[0:1] + z * w_ref[1:2] + z_up * w_ref[2:3] + b_ref[...]
    c = D_HYENA
    x0_ref[0] = y[:, :c].astype(BF16)
    u_ref[0] = (y[:, c:2 * c] * y[:, 2 * c:]).astype(BF16)


def _hyena_prep(z, conv_w, conv_b):
    nb, n, c3 = z.shape
    tm = PROJ_TM
    r = tm // 8
    nblk8 = n // 8
    in_specs = [pl.BlockSpec((1, tm, c3), lambda b, i: (b, i, 0)),
                pl.BlockSpec((1, 8, c3), lambda b, i: (b, jnp.maximum(i * r - 1, 0), 0)),
                pl.BlockSpec((1, 8, c3), lambda b, i: (b, jnp.minimum((i + 1) * r, nblk8 - 1), 0)),
                _const_spec(conv_w), pl.BlockSpec((1, c3), lambda b, i: (0, 0))]
    return pl.pallas_call(
        _hyena_prep_kernel,
        grid=(nb, n // tm),
        in_specs=in_specs,
        out_specs=(pl.BlockSpec((1, tm, D_HYENA), lambda b, i: (b, i, 0)),
                   pl.BlockSpec((1, tm, D_HYENA), lambda b, i: (b, i, 0))),
        out_shape=(jax.ShapeDtypeStruct((nb, n, D_HYENA), BF16),
                   jax.ShapeDtypeStruct((nb, n, D_HYENA), BF16)),
        compiler_params=_cparams("parallel", "parallel"),
        name="hyena_prep",
    )(z, z, z, conv_w, conv_b.reshape(1, c3))


def _filter_kernel(t_ref, om_ref, band_ref, w1_ref, b1_ref, wi_ref, bi_ref, fr_ref, wo_ref, dl_ref, h_ref):
    t = t_ref[...]
    lane = lax.broadcasted_iota(jnp.int32, (t.shape[0], LANES), 1)
    ang = om_ref[...] * band_ref[...]
    nb = FILTER_EMB_BANDS
    emb = jnp.where(lane == 0, t,
                    jnp.where(lane <= nb, jnp.cos(ang),
                              jnp.where(lane <= 2 * nb, -jnp.sin(ang), 0.0)))
    fr = fr_ref[...]
    h = jnp.sin(fr * (_dot3(emb, w1_ref[...]) + b1_ref[...]))
    for i in range(FILTER_INNER):
        h = jnp.sin(fr * (_dot3(h, wi_ref[i]) + bi_ref[i]))
    h = _dot3(h, wo_ref[...]) * jnp.exp(-t * dl_ref[...])
    c = D_HYENA
    h_ref[0] = h[:, :c]
    first = (pl.program_id(0) == 0) & (lax.broadcasted_iota(jnp.int32, (t.shape[0], c), 0) == 0)
    h_ref[1] = jnp.where(first, 0.0, h[:, c:])


def _hyena_filter(n, w1, b1, w_inner, b_inner, freq, w_out):
    tl = min(FILT_TL, n)
    t = jnp.linspace(0.0, 1.0, n, dtype=F32)[:, None]
    omega = (2.0 * math.pi / n) * jnp.arange(n, dtype=F32)[:, None]
    bands = jnp.linspace(1e-4, FILTER_EMB_BANDS - 1, FILTER_EMB_BANDS, dtype=F32)
    pad = LANES - 1 - 2 * FILTER_EMB_BANDS
    band_row = jnp.concatenate([jnp.zeros((1,), F32), bands, bands, jnp.zeros((pad,), F32)])[None, :]
    w1p = jnp.concatenate([w1.astype(F32), jnp.zeros((LANES - w1.shape[0], w1.shape[1]), F32)], axis=0)
    min_decay = math.log(DECAY_TARGET) / SLOW_DECAY_PCT
    max_decay = math.log(DECAY_TARGET) / FAST_DECAY_PCT
    deltas = jnp.abs(jnp.linspace(min_decay, max_decay, D_HYENA, dtype=F32))
    dl = jnp.concatenate([deltas, deltas])[None, :]
    fh = FILTER_HIDDEN
    args = [t, omega, band_row, w1p, b1.reshape(1, fh), w_inner, b_inner.reshape(FILTER_INNER, 1, fh),
            freq.reshape(1, fh), w_out, dl]

    def cs(a):
        nd = a.ndim
        return pl.BlockSpec(a.shape, lambda i: (0,) * nd)

    in_specs = [pl.BlockSpec((tl, 1), lambda i: (i, 0)), pl.BlockSpec((tl, 1), lambda i: (i, 0))]
    in_specs += [cs(a) for a in args[2:]]
    return pl.pallas_call(
        _filter_kernel,
        grid=(n // tl,),
        in_specs=in_specs,
        out_specs=pl.BlockSpec((2, tl, D_HYENA), lambda i: (0, i, 0)),
        out_shape=jax.ShapeDtypeStruct((2, n, D_HYENA), F32),
        compiler_params=_cparams("parallel"),
        name="hyena_filter",
    )(*args)


def _dft_tables(n):
    m2 = FFT_M2
    m1 = 2 * n // m2
    m = 2 * n
    k1 = np.arange(m1, dtype=np.float64)[:, None]
    n1 = np.arange(m1 // 2, dtype=np.float64)[None, :]
    ang1 = 2.0 * np.pi * k1 * n1 / m1
    f1 = np.concatenate([np.cos(ang1), -np.sin(ang1)], axis=0)
    finv = np.concatenate([np.cos(ang1).T, -np.sin(ang1).T], axis=1)
    a = np.arange(m2, dtype=np.float64)
    ang2 = 2.0 * np.pi * a[:, None] * a[None, :] / m2
    angt = 2.0 * np.pi * k1 * a[None, :] / m
    return dict(
        m1=m1,
        f1=jnp.asarray(f1, F32), finv=jnp.asarray(finv, F32),
        f2r=jnp.asarray(np.cos(ang2), F32), f2i=jnp.asarray(-np.sin(ang2), F32),
        twr=jnp.asarray(np.cos(angt)[:, None, :], F32), twi=jnp.asarray(-np.sin(angt)[:, None, :], F32))


def _dft_a_kernel(x_ref, f_ref, o_ref):
    o_ref[0] = _dot(f_ref[...].astype(BF16), x_ref[0].astype(BF16)).astype(BF16)


def _dft_stage_a(x, f1):
    g, r, w = x.shape
    tn = min(FFT_TN, w)
    rows = f1.shape[0]
    return pl.pallas_call(
        _dft_a_kernel,
        grid=(g, w // tn),
        in_specs=[pl.BlockSpec((1, r, tn), lambda b, j: (b, 0, j)),
                  pl.BlockSpec(f1.shape, lambda b, j: (0, 0))],
        out_specs=pl.BlockSpec((1, rows, tn), lambda b, j: (b, 0, j)),
        out_shape=jax.ShapeDtypeStruct((g, rows, w), BF16),
        compiler_params=_cparams("parallel", "parallel"),
        name="dft_stage_a",
    )(x, f1)


def _twiddled_dft(f2r, f2i, twr, twi):
    gr = f2r * twr - f2i * twi
    gi = f2r * twi + f2i * twr
    return jnp.concatenate([jnp.concatenate([gr, -gi], axis=1),
                            jnp.concatenate([gi, gr], axis=1)], axis=0)


def _filter_spectrum_kernel(a_ref, f2r_ref, f2i_ref, twr_ref, twi_ref, k_ref, *, kc):
    m2 = FFT_M2
    for j in range(kc):
        g = _twiddled_dft(f2r_ref[...], f2i_ref[...], twr_ref[j], twi_ref[j]).astype(BF16)
        xf = _dot(g, a_ref[0, :, j].reshape(2 * m2, -1))
        xb = _dot(g, a_ref[1, :, j].reshape(2 * m2, -1))
        k_ref[0, j] = xf[:m2] + xb[:m2]
        k_ref[1, j] = xf[m2:] - xb[m2:]


def _filter_spectrum(a5, tabs):
    _, _, m1, m2, c = a5.shape
    kc = FFT_KC
    in_specs = [pl.BlockSpec((2, 2, kc, m2, c), lambda i: (0, 0, i, 0, 0)),
                pl.BlockSpec((m2, m2), lambda i: (0, 0)), pl.BlockSpec((m2, m2), lambda i: (0, 0)),
                pl.BlockSpec((kc, 1, m2), lambda i: (i, 0, 0)), pl.BlockSpec((kc, 1, m2), lambda i: (i, 0, 0))]
    return pl.pallas_call(
        functools.partial(_filter_spectrum_kernel, kc=kc),
        grid=(m1 // kc,),
        in_specs=in_specs,
        out_specs=pl.BlockSpec((2, kc, m2, c), lambda i: (0, i, 0, 0)),
        out_shape=jax.ShapeDtypeStruct((2, m1, m2, c), F32),
        compiler_params=_cparams("parallel"),
        name="filter_spectrum",
    )(a5, tabs["f2r"], tabs["f2i"], tabs["twr"], tabs["twi"])


def _spectral_kernel(a_ref, kf_ref, f2r_ref, f2i_ref, twr_ref, twi_ref, b_ref, *, kc, nb):
    m2 = FFT_M2
    for j in range(kc):
        g32 = _twiddled_dft(f2r_ref[...], f2i_ref[...], twr_ref[j], twi_ref[j])
        g = g32.astype(BF16)
        gt = g32.T.astype(BF16)
        kr = kf_ref[0, j]
        ki = kf_ref[1, j]
        for b in range(nb):
            x = _dot(g, a_ref[b, :, j].reshape(2 * m2, -1))
            xr, xi = x[:m2], x[m2:]
            y = jnp.concatenate([xr * kr - xi * ki, xr * ki + xi * kr], axis=0).astype(BF16)
            z = _dot(gt, y)
            b_ref[b, :, j] = z.reshape(2, m2, -1).astype(BF16)


def _spectral_multiply(a5, kf, tabs):
    nb, _, m1, m2, c = a5.shape
    kc = FFT_KC
    in_specs = [pl.BlockSpec((nb, 2, kc, m2, c), lambda i: (0, 0, i, 0, 0)),
                pl.BlockSpec((2, kc, m2, c), lambda i: (0, i, 0, 0)),
                pl.BlockSpec((m2, m2), lambda i: (0, 0)), pl.BlockSpec((m2, m2), lambda i: (0, 0)),
                pl.BlockSpec((kc, 1, m2), lambda i: (i, 0, 0)), pl.BlockSpec((kc, 1, m2), lambda i: (i, 0, 0))]
    return pl.pallas_call(
        functools.partial(_spectral_kernel, kc=kc, nb=nb),
        grid=(m1 // kc,),
        in_specs=in_specs,
        out_specs=pl.BlockSpec((nb, 2, kc, m2, c), lambda i: (0, 0, i, 0, 0)),
        out_shape=jax.ShapeDtypeStruct((nb, 2, m1, m2, c), BF16),
        compiler_params=_cparams("parallel"),
        name="spectral_multiply",
    )(a5, kf, tabs["f2r"], tabs["f2i"], tabs["twr"], tabs["twi"])


def _idft_a_kernel(b_ref, f_ref, x0_ref, u_ref, bias_ref, y_ref, *, inv_m):
    conv = _dot(f_ref[...].astype(BF16), b_ref[0]) * inv_m
    u = u_ref[0].astype(F32)
    y_ref[0] = (x0_ref[0].astype(F32) * (conv + u * bias_ref[...])).astype(BF16)


def _idft_stage_a(bm, finv, x0v, uv, bias_row, inv_m):
    g, rows, w = bm.shape
    r = finv.shape[0]
    tn = min(FFT_TN, w)
    return pl.pallas_call(
        functools.partial(_idft_a_kernel, inv_m=inv_m),
        grid=(g, w // tn),
        in_specs=[pl.BlockSpec((1, rows, tn), lambda b, j: (b, 0, j)),
                  pl.BlockSpec(finv.shape, lambda b, j: (0, 0)),
                  pl.BlockSpec((1, r, tn), lambda b, j: (b, 0, j)),
                  pl.BlockSpec((1, r, tn), lambda b, j: (b, 0, j)),
                  pl.BlockSpec((1, tn), lambda b, j: (0, 0))],
        out_specs=pl.BlockSpec((1, r, tn), lambda b, j: (b, 0, j)),
        out_shape=jax.ShapeDtypeStruct((g, r, w), BF16),
        compiler_params=_cparams("parallel", "parallel"),
        name="idft_stage_a",
    )(bm, finv, x0v, uv, bias_row)


def _hyena_mixer(z, conv_w, conv_b, w1, b1, w_inner, b_inner, freq, w_out, bias):
    nb, n, _ = z.shape
    c = D_HYENA
    tabs = _dft_tables(n)
    m1, m2 = tabs["m1"], FFT_M2
    w = m2 * c
    u, x0 = _hyena_prep(z, conv_w, conv_b)
    h = _hyena_filter(n, w1, b1, w_inner, b_inner, freq, w_out)
    ah = _dft_stage_a(h.reshape(2, m1 // 2, w), tabs["f1"])
    kf = _filter_spectrum(ah.reshape(2, 2, m1, m2, c), tabs)
    au = _dft_stage_a(u.reshape(nb, m1 // 2, w), tabs["f1"])
    bm = _spectral_multiply(au.reshape(nb, 2, m1, m2, c), kf, tabs)
    tn = min(FFT_TN, w)
    bias_row = jnp.tile(bias.astype(F32), tn // c)[None, :]
    y = _idft_stage_a(bm.reshape(nb, 2 * m1, w), tabs["finv"], x0.reshape(nb, m1 // 2, w),
                      u.reshape(nb, m1 // 2, w), bias_row, 1.0 / (2 * n))
    return y.reshape(nb, n, c)


def _outproj_kernel(h_ref, yh_ref, yd_ref, wa_ref, wb_ref, gt_ref, o_ref):
    mix = _dot(yh_ref[0], wa_ref[...]) + _dot(yd_ref[0], wb_ref[...])
    o_ref[0] = h_ref[0] + gt_ref[0] * mix


def _output_projection(h, y_hy, y_diff, wo_a, wo_b, gate):
    nb, n, d = h.shape
    tm = PROJ_TM
    in_specs = [pl.BlockSpec((1, tm, d), lambda b, i: (b, i, 0)),
                pl.BlockSpec((1, tm, D_HYENA), lambda b, i: (b, i, 0)),
                pl.BlockSpec((1, tm, D_DIFF), lambda b, i: (b, i, 0)),
                _const_spec(wo_a), _const_spec(wo_b), _mod_spec(gate, nb)]
    return pl.pallas_call(
        _outproj_kernel,
        grid=(nb, n // tm),
        in_specs=in_specs,
        out_specs=pl.BlockSpec((1, tm, d), lambda b, i: (b, i, 0)),
        out_shape=jax.ShapeDtypeStruct((nb, n, d), F32),
        compiler_params=_cparams("parallel", "parallel"),
        name="output_projection",
    )(h, y_hy, y_diff, wo_a, wo_b, gate)


def _rope_tables(n):
    rows = n // GRID_W
    row = jnp.broadcast_to(jnp.arange(rows, dtype=F32)[:, None], (rows, GRID_W)).reshape(-1)
    col = jnp.broadcast_to(jnp.arange(GRID_W, dtype=F32)[None, :], (rows, GRID_W)).reshape(-1)
    axis_dim = DIFF_HEAD_DIM // 2
    inv_freq = ROPE_BASE ** (-jnp.arange(0, axis_dim, 2, dtype=F32) / axis_dim)
    ang_r = row[:, None] * inv_freq
    ang_c = col[:, None] * inv_freq
    cos64 = jnp.concatenate([jnp.cos(ang_r), jnp.cos(ang_r), jnp.cos(ang_c), jnp.cos(ang_c)], axis=1)
    sin64 = jnp.concatenate([-jnp.sin(ang_r), jnp.sin(ang_r), -jnp.sin(ang_c), jnp.sin(ang_c)], axis=1)
    reps = LANES // DIFF_HEAD_DIM
    return (jnp.tile(cos64, (1, reps)), jnp.tile(sin64, (1, reps)), cos64.T, sin64.T)


def _chunk_cols(w, tf):
    d, f = w.shape
    return w.reshape(d, f // tf, tf).transpose(1, 0, 2).astype(BF16)


def kernel(x, c, ctx, c_ctx, ada_w, ada_b, norm_g, ffn_w_gate, ffn_w_up, ffn_w_down, w_in, w_out,
           hyena_conv_w, hyena_conv_b, filt_w1, filt_b1, filt_w_inner, filt_b_inner, filt_sin_freq,
           filt_w_out, hyena_bias, diff_lambda, diff_subln_g, final_g):
    nb, n, d = x.shape
    layer = 0
    hy_end = HYENA_N_PROJ * D_HYENA
    q_end = hy_end + D_DIFF
    k_end = q_end + D_DIFF

    cc = jnp.concatenate([c, c_ctx[None, :], jnp.zeros((8 - nb - 1, d), F32)], axis=0)
    mod = _modulation(cc, ada_w[layer], ada_b[layer])
    mod_lat = [mod[:nb, j * d:(j + 1) * d][:, None, :] for j in range(N_ADA)]
    mod_ctx = [mod[nb:nb + 1, j * d:(j + 1) * d][:, None, :] for j in range(N_ADA)]

    def ffn_weights(i):
        return (_chunk_cols(ffn_w_gate[layer, i], FFN_TF), _chunk_cols(ffn_w_up[layer, i], FFN_TF),
                ffn_w_down[layer, i].reshape(FFN_HIDDEN // FFN_TF, FFN_TF, d).astype(BF16))

    ffn0 = ffn_weights(0)
    h_lat = _ffn_half_step(x, norm_g[layer, 0], mod_lat[0], mod_lat[1], mod_lat[2], *ffn0)
    h_ctx = _ffn_half_step(ctx, norm_g[layer, 0], mod_ctx[0], mod_ctx[1], mod_ctx[2], *ffn0)

    w = w_in[layer]
    wzk = jnp.concatenate([w[:, :hy_end], w[:, q_end:k_end]], axis=1).astype(BF16)
    wqv = jnp.concatenate([w[:, hy_end:q_end], w[:, k_end:]], axis=1).T.astype(BF16)
    ck, sk, cq, sq = _rope_tables(n)
    z, k_lat, qT, vT4 = _input_projection(h_lat, norm_g[layer, 1], mod_lat[3], mod_lat[4],
                                          wzk, wqv, ck, sk, cq, sq)
    k_ctx, vT_ctx = _context_projection(h_ctx, norm_g[layer, 1], mod_ctx[3], mod_ctx[4],
                                        w[:, q_end:k_end].astype(BF16), w[:, k_end:].T.astype(BF16))

    y_diff = _diff_attention(qT, k_lat, vT4, k_ctx, vT_ctx, diff_lambda[layer].astype(F32),
                             diff_subln_g[layer])
    y_hy = _hyena_mixer(z, hyena_conv_w[layer], hyena_conv_b[layer], filt_w1[layer], filt_b1[layer],
                        filt_w_inner[layer], filt_b_inner[layer], filt_sin_freq[layer],
                        filt_w_out[layer], hyena_bias[layer])

    wo = w_out[layer].astype(BF16)
    h_lat = _output_projection(h_lat, y_hy, y_diff, wo[:D_HYENA], wo[D_HYENA:], mod_lat[5])
    return _ffn_half_step(h_lat, norm_g[layer, 2], mod_lat[6], mod_lat[7], mod_lat[8],
                          *ffn_weights(1), final_g=final_g)
```

```python
import functools
import math

import numpy as np
import jax
import jax.numpy as jnp
from jax import lax
from jax.experimental import pallas as pl
from jax.experimental.pallas import tpu as pltpu

F32 = jnp.float32
BF16 = jnp.bfloat16

N_ADA = 9
FFN_HIDDEN = 2816
D_HYENA = 512
HYENA_N_PROJ = 3
FILTER_EMB_BANDS = 16
FILTER_HIDDEN = 64
FILTER_EMB_ROWS = 40
FILTER_INNER = 2
DECAY_TARGET = 1e-2
FAST_DECAY_PCT = 0.3
SLOW_DECAY_PCT = 1.5
N_DIFF_HEADS = 4
DIFF_HEAD_DIM = 64
D_DIFF = N_DIFF_HEADS * 2 * DIFF_HEAD_DIM
GRID_W = 64
ROPE_BASE = 10000.0
RMS_EPS = 1e-6
SUBLN_EPS = 1e-5
LAM_INIT = 0.8 - 0.6 * math.exp(-0.3 * 0)

LANES = 128
BF16_ROWS = 16
V7X_VMEM_BYTES = 64 * 2**20
VMEM_LIMIT = 56 * 2**20

FFN_TM = 512
FFN_TF = 256
PROJ_TM = 512
ATT_TK = 512
ATT_TQ = 1024
FFT_M2 = 128
FFT_TN = 4096
FFT_KC = 8
FILT_TL = 1024
NEG_BIG = -1e30

LOG2E = 1.4426950408889634


def _cparams(*sem):
    return pltpu.CompilerParams(dimension_semantics=sem, vmem_limit_bytes=VMEM_LIMIT)


def _dot(a, b):
    return jnp.dot(a, b, preferred_element_type=F32)


def _dot3(a, b):
    a_hi = a.astype(BF16)
    a_lo = (a - a_hi.astype(F32)).astype(BF16)
    b_hi = b.astype(BF16)
    b_lo = (b - b_hi.astype(F32)).astype(BF16)
    return _dot(a_hi, b_hi) + (_dot(a_lo, b_hi) + _dot(a_hi, b_lo))


def _silu(x):
    return x * (1.0 / (1.0 + jnp.exp(-x)))


def _adaln(s, g, shift, scale):
    ms = jnp.mean(s * s, axis=-1, keepdims=True)
    return (s * lax.rsqrt(ms + RMS_EPS) * g) * (1.0 + scale) + shift


def _mod_kernel(c_ref, w_ref, b_ref, o_ref):
    o_ref[...] = _dot3(_silu(c_ref[...]), w_ref[...]) + b_ref[...]


def _modulation(cc, ada_w, ada_b):
    rows, d = cc.shape
    n = ada_w.shape[1]
    tn = 768
    return pl.pallas_call(
        _mod_kernel,
        grid=(n // tn,),
        in_specs=[pl.BlockSpec((rows, d), lambda j: (0, 0)),
                  pl.BlockSpec((d, tn), lambda j: (0, j)),
                  pl.BlockSpec((1, tn), lambda j: (0, j))],
        out_specs=pl.BlockSpec((rows, tn), lambda j: (0, j)),
        out_shape=jax.ShapeDtypeStruct((rows, n), F32),
        compiler_params=_cparams("parallel"),
        name="modulation",
    )(cc, ada_w, ada_b.reshape(1, n))


def _mod_spec(arr, nb):
    d = arr.shape[-1]
    if arr.shape[0] == nb:
        return pl.BlockSpec((1, 1, d), lambda b, i: (b, 0, 0))
    return pl.BlockSpec((1, 1, d), lambda b, i: (0, 0, 0))


def _const_spec(arr):
    nd = arr.ndim
    return pl.BlockSpec(arr.shape, lambda b, i: (0,) * nd)


def _ffn_kernel(*refs, n_chunks, tf, mix, final):
    refs = list(refs)
    o_ref = refs.pop()
    s_ref, g_ref, sh_ref, sc_ref, gt_ref, wg_ref, wu_ref, wd_ref = refs[:8]
    rest = refs[8:]
    s = s_ref[0]
    if mix:
        yh_ref, yd_ref, woa_ref, wob_ref, gm_ref = rest[:5]
        rest = rest[5:]
        s = s + gm_ref[0] * (_dot(yh_ref[0], woa_ref[...]) + _dot(yd_ref[0], wob_ref[...]))
    ub = _adaln(s, g_ref[...], sh_ref[0], sc_ref[0]).astype(BF16)
    acc = jnp.zeros(s.shape, F32)
    for f in range(n_chunks):
        cols = slice(f * tf, (f + 1) * tf)
        gate = _dot(ub, wg_ref[:, cols])
        up = _dot(ub, wu_ref[:, cols])
        a = (_silu(gate) * up).astype(BF16)
        acc = acc + _dot(a, wd_ref[cols, :])
    out = s + (0.5 * gt_ref[0]) * acc
    if final:
        ms = jnp.mean(out * out, axis=-1, keepdims=True)
        out = out * lax.rsqrt(ms + RMS_EPS) * rest[0][...]
    o_ref[0] = out


def _ffn_half_step(s, g, shift, scale, gate, wg, wu, wd, mixer=None, final_g=None):
    nb, t, d = s.shape
    tm = min(FFN_TM, t)
    args = [s, g.reshape(1, d), shift, scale, gate, wg, wu, wd]
    in_specs = [pl.BlockSpec((1, tm, d), lambda b, i: (b, i, 0)),
                _const_spec(args[1]),
                _mod_spec(shift, nb), _mod_spec(scale, nb), _mod_spec(gate, nb),
                _const_spec(wg), _const_spec(wu), _const_spec(wd)]
    if mixer is not None:
        yh, yd, woa, wob, gm = mixer
        args += [yh, yd, woa, wob, gm]
        in_specs += [pl.BlockSpec((1, tm, yh.shape[2]), lambda b, i: (b, i, 0)),
                     pl.BlockSpec((1, tm, yd.shape[2]), lambda b, i: (b, i, 0)),
                     _const_spec(woa), _const_spec(wob), _mod_spec(gm, nb)]
    if final_g is not None:
        args.append(final_g.reshape(1, d))
        in_specs.append(_const_spec(args[-1]))
    kern = functools.partial(_ffn_kernel, n_chunks=wg.shape[1] // FFN_TF, tf=FFN_TF,
                             mix=mixer is not None, final=final_g is not None)
    return pl.pallas_call(
        kern,
        grid=(nb, t // tm),
        in_specs=in_specs,
        out_specs=pl.BlockSpec((1, tm, d), lambda b, i: (b, i, 0)),
        out_shape=jax.ShapeDtypeStruct((nb, t, d), F32),
        compiler_params=_cparams("parallel", "parallel"),
        name="ffn_mix_final" if mixer is not None else "ffn",
    )(*args)


def _rope_partner(x, axis):
    n = x.shape[axis]
    idx = lax.broadcasted_iota(jnp.int32, x.shape, axis)
    first_half = (idx & 16) == 0
    return jnp.where(first_half, pltpu.roll(x, n - 16, axis), pltpu.roll(x, 16, axis))


def _inproj_kernel(s_ref, g_ref, sh_ref, sc_ref, wzk_ref, wqv_ref, ck_ref, sk_ref, cq_ref, sq_ref,
                   z_ref, k_ref, qT_ref, vT_ref, *, q_scale):
    ub = _adaln(s_ref[0], g_ref[...], sh_ref[0], sc_ref[0]).astype(BF16)
    tm = ub.shape[0]
    zk = _dot(ub, wzk_ref[...])
    hy = HYENA_N_PROJ * D_HYENA
    z_ref[0] = zk[:, :hy].astype(BF16)
    k = zk[:, hy:]
    reps = D_DIFF // LANES
    ck = jnp.concatenate([ck_ref[...]] * reps, axis=1)
    sk = jnp.concatenate([sk_ref[...]] * reps, axis=1)
    k_ref[0] = (k * ck + _rope_partner(k, 1) * sk).astype(BF16)
    qv = lax.dot_general(wqv_ref[...], ub, (((1,), (1,)), ((), ())), preferred_element_type=F32)
    q = qv[:D_DIFF]
    groups = D_DIFF // DIFF_HEAD_DIM
    cq = jnp.broadcast_to(cq_ref[...][None], (groups, DIFF_HEAD_DIM, tm)).reshape(D_DIFF, tm)
    sq = jnp.broadcast_to(sq_ref[...][None], (groups, DIFF_HEAD_DIM, tm)).reshape(D_DIFF, tm)
    q = (q * cq + _rope_partner(q, 0) * sq) * q_scale
    qT_ref[0] = q.astype(BF16)
    v = qv[D_DIFF:].astype(BF16)
    tk = vT_ref.shape[3]
    for j in range(vT_ref.shape[1]):
        vT_ref[0, j] = v[:, j * tk:(j + 1) * tk]


def _input_projection(h, g, shift, scale, wzk, wqv, ck, sk, cq, sq):
    nb, n, d = h.shape
    tm = PROJ_TM
    hy = HYENA_N_PROJ * D_HYENA
    q_scale = DIFF_HEAD_DIM ** -0.5 * LOG2E
    args = [h, g.reshape(1, d), shift, scale, wzk, wqv, ck, sk, cq, sq]
    in_specs = [pl.BlockSpec((1, tm, d), lambda b, i: (b, i, 0)),
                _const_spec(args[1]), _mod_spec(shift, nb), _mod_spec(scale, nb),
                _const_spec(wzk), _const_spec(wqv),
                pl.BlockSpec((tm, LANES), lambda b, i: (i, 0)),
                pl.BlockSpec((tm, LANES), lambda b, i: (i, 0)),
                pl.BlockSpec((DIFF_HEAD_DIM, tm), lambda b, i: (0, i)),
                pl.BlockSpec((DIFF_HEAD_DIM, tm), lambda b, i: (0, i))]
    out_shape = (jax.ShapeDtypeStruct((nb, n, hy), BF16),
                 jax.ShapeDtypeStruct((nb, n, D_DIFF), BF16),
                 jax.ShapeDtypeStruct((nb, D_DIFF, n), BF16),
                 jax.ShapeDtypeStruct((nb, n // ATT_TK, D_DIFF, ATT_TK), BF16))
    out_specs = (pl.BlockSpec((1, tm, hy), lambda b, i: (b, i, 0)),
                 pl.BlockSpec((1, tm, D_DIFF), lambda b, i: (b, i, 0)),
                 pl.BlockSpec((1, D_DIFF, tm), lambda b, i: (b, 0, i)),
                 pl.BlockSpec((1, tm // ATT_TK, D_DIFF, ATT_TK), lambda b, i: (b, i, 0, 0)))
    return pl.pallas_call(
        functools.partial(_inproj_kernel, q_scale=q_scale),
        grid=(nb, n // tm),
        in_specs=in_specs, out_specs=out_specs, out_shape=out_shape,
        compiler_params=_cparams("parallel", "parallel"),
        name="input_projection",
    )(*args)


def _inproj_ctx_kernel(s_ref, g_ref, sh_ref, sc_ref, wk_ref, wv_ref, k_ref, vT_ref):
    ub = _adaln(s_ref[0], g_ref[...], sh_ref[0], sc_ref[0]).astype(BF16)
    k_ref[0] = _dot(ub, wk_ref[...]).astype(BF16)
    vT_ref[0] = lax.dot_general(wv_ref[...], ub, (((1,), (1,)), ((), ())),
                                preferred_element_type=F32).astype(BF16)


def _context_projection(h, g, shift, scale, wk, wvT):
    nb, c, d = h.shape
    args = [h, g.reshape(1, d), shift, scale, wk, wvT]
    in_specs = [pl.BlockSpec((1, c, d), lambda b, i: (b, 0, 0)),
                _const_spec(args[1]), _mod_spec(shift, nb), _mod_spec(scale, nb),
                _const_spec(wk), _const_spec(wvT)]
    return pl.pallas_call(
        _inproj_ctx_kernel,
        grid=(nb, 1),
        in_specs=in_specs,
        out_specs=(pl.BlockSpec((1, c, D_DIFF), lambda b, i: (b, 0, 0)),
                   pl.BlockSpec((1, D_DIFF, c), lambda b, i: (b, 0, 0))),
        out_shape=(jax.ShapeDtypeStruct((nb, c, D_DIFF), BF16),
                   jax.ShapeDtypeStruct((nb, D_DIFF, c), BF16)),
        compiler_params=_cparams("parallel", "parallel"),
        name="context_projection",
    )(*args)


def _attn_kernel(qT_ref, k_ref, vT_ref, kc_ref, vTc_ref, lp_ref, sg_ref, o_ref,
                 acc0_ref, acc1_ref, qa_ref, qb_ref, *sp_refs, n_chunks, tk):
    d = DIFF_HEAD_DIM
    s_refs = (sp_refs[0:2], sp_refs[2:4])
    p_refs = (sp_refs[4:6], sp_refs[6:8])
    q = qT_ref[0]
    tq = q.shape[1]
    row = lax.broadcasted_iota(jnp.int32, q.shape, 0)
    zero = jnp.zeros_like(q)
    qa_ref[...] = jnp.where(row < d, q, zero)
    qb_ref[...] = jnp.where(row < d, zero, q)
    acc0_ref[...] = jnp.zeros_like(acc0_ref)
    acc1_ref[...] = jnp.zeros_like(acc1_ref)
    q_refs = (qa_ref, qb_ref)
    acc_refs = (acc0_ref, acc1_ref)

    def with_ones(vc):
        r = lax.broadcasted_iota(jnp.int32, (BF16_ROWS, vc.shape[1]), 0)
        return jnp.concatenate([vc, jnp.where(r == 0, 1.0, 0.0).astype(BF16)], axis=0)

    def scores(kc, slot):
        cmax = []
        for h in range(2):
            s = _dot(kc, q_refs[h][...])
            s_refs[slot][h][...] = s
            cmax.append(jnp.max(s, axis=0, keepdims=True))
        return tuple(cmax)

    def probs(s, cmax, m):
        m_new = jnp.maximum(m, cmax)
        return jnp.exp2((s - m_new).astype(BF16)), m_new, jnp.exp2(m - m_new)

    def softmax_stage(slot, cmax, stats):
        out = []
        for h in range(2):
            p, m, alpha = probs(s_refs[slot][h][...], cmax[h], stats[h][0])
            p_refs[slot][h][...] = p
            out.append((m, alpha))
        return tuple(out)

    def values(vc, slot, stats):
        for h in range(2):
            acc_refs[h][...] = stats[h][1] * acc_refs[h][...] + _dot(vc, p_refs[slot][h][...])

    def lat_keys(t):
        return k_ref[0, pl.ds(pl.multiple_of(t * tk, tk), tk), :]

    def step(t, slot, cmax_other, stats, with_scores=True):
        cmax = scores(lat_keys(t + 2), slot) if with_scores else None
        values(with_ones(vT_ref[0, t]), slot, stats)
        new_stats = softmax_stage(1 - slot, cmax_other, stats)
        return cmax, new_stats

    neg = jnp.full((1, tq), NEG_BIG, F32)
    one = jnp.ones((1, tq), F32)
    cmax0 = scores(lat_keys(0), 0)
    cmax1 = scores(lat_keys(1), 1)
    carry = (cmax1, softmax_stage(0, cmax0, ((neg, one), (neg, one))))

    def body(i, carry):
        cmax0, stats = step(2 * i, 0, *carry)
        return step(2 * i + 1, 1, cmax0, stats)

    carry = lax.fori_loop(0, (n_chunks - 2) // 2, body, carry)
    _, stats = step(n_chunks - 2, 0, *carry, with_scores=False)
    values(with_ones(vT_ref[0, n_chunks - 1]), 1, stats)

    kc = kc_ref[0]
    vc = with_ones(vTc_ref[0])
    for h in range(2):
        s = _dot(kc, q_refs[h][...])
        p, _, alpha = probs(s, jnp.max(s, axis=0, keepdims=True), stats[h][0])
        acc_refs[h][...] = alpha * acc_refs[h][...] + _dot(vc, p)

    lp = lp_ref[...]
    lam = (jnp.exp(jnp.sum(lp[0:1] * lp[1:2], axis=1, keepdims=True))
           - jnp.exp(jnp.sum(lp[2:3] * lp[3:4], axis=1, keepdims=True)) + LAM_INIT)
    hd = 2 * d
    o0 = acc0_ref[:hd, :] / acc0_ref[hd:hd + 1, :]
    o1 = acc1_ref[:hd, :] / acc1_ref[hd:hd + 1, :]
    oT = o0 - lam * o1
    ms = jnp.mean(oT * oT, axis=0, keepdims=True)
    oT = oT * lax.rsqrt(ms + SUBLN_EPS)
    o_ref[0] = ((oT.T * sg_ref[...]) * (1.0 - LAM_INIT)).astype(BF16)


def _diff_attention(qT, k, vT4, k_ctx, vT_ctx, lam_params, subln_g):
    nb, n, _ = k.shape
    hd = 2 * DIFF_HEAD_DIM
    n_chunks, tk = vT4.shape[1], vT4.shape[3]
    c = k_ctx.shape[1]
    tq = ATT_TQ
    in_specs = [pl.BlockSpec((1, hd, tq), lambda b, h, i: (b, h, i)),
                pl.BlockSpec((1, n, hd), lambda b, h, i: (b, 0, h)),
                pl.BlockSpec((1, n_chunks, hd, tk), lambda b, h, i: (b, 0, h, 0)),
                pl.BlockSpec((1, c, hd), lambda b, h, i: (b, 0, h)),
                pl.BlockSpec((1, hd, c), lambda b, h, i: (b, h, 0)),
                pl.BlockSpec(lam_params.shape, lambda b, h, i: (0, 0)),
                pl.BlockSpec((1, hd), lambda b, h, i: (0, 0))]
    return pl.pallas_call(
        functools.partial(_attn_kernel, n_chunks=n_chunks, tk=tk),
        grid=(nb, N_DIFF_HEADS, n // tq),
        in_specs=in_specs,
        out_specs=pl.BlockSpec((1, tq, hd), lambda b, h, i: (b, i, h)),
        out_shape=jax.ShapeDtypeStruct((nb, n, D_DIFF), BF16),
        scratch_shapes=([pltpu.VMEM((hd + BF16_ROWS, tq), F32)] * 2 + [pltpu.VMEM((hd, tq), BF16)] * 2
                        + [pltpu.VMEM((tk, tq), F32)] * 4 + [pltpu.VMEM((tk, tq), BF16)] * 4),
        compiler_params=_cparams("parallel", "parallel", "arbitrary"),
        name="diff_attention",
    )(qT, k, vT4, k_ctx, vT_ctx, lam_params, subln_g.reshape(1, hd))


def _hyena_prep_kernel(z_ref, zp_ref, zn_ref, w_ref, b_ref, u_ref, x0_ref):
    i = pl.program_id(1)
    last = pl.num_programs(1) - 1
    z = z_ref[0].astype(F32)
    tm = z.shape[0]
    prev = jnp.where(i == 0, 0.0, zp_ref[0, 7:8, :].astype(F32))
    nxt = jnp.where(i == last, 0.0, zn_ref[0, 0:1, :].astype(F32))
    row = lax.broadcasted_iota(jnp.int32, z.shape, 0)
    z_dn = jnp.where(row == 0, prev, pltpu.roll(z, 1, 0))
    z_up = jnp.where(row == tm - 1, nxt, pltpu.roll(z, tm - 1, 0))
    y = z_dn * w_ref[0:1] + z * w_ref[1:2] + z_up * w_ref[2:3] + b_ref[...]
    c = D_HYENA
    x0_ref[0] = y[:, :c].astype(BF16)
    u_ref[0] = (y[:, c:2 * c] * y[:, 2 * c:]).astype(BF16)


def _hyena_prep(z, conv_w, conv_b):
    nb, n, c3 = z.shape
    tm = PROJ_TM
    r = tm // 8
    nblk8 = n // 8
    in_specs = [pl.BlockSpec((1, tm, c3), lambda b, i: (b, i, 0)),
                pl.BlockSpec((1, 8, c3), lambda b, i: (b, jnp.maximum(i * r - 1, 0), 0)),
                pl.BlockSpec((1, 8, c3), lambda b, i: (b, jnp.minimum((i + 1) * r, nblk8 - 1), 0)),
                _const_spec(conv_w), pl.BlockSpec((1, c3), lambda b, i: (0, 0))]
    return pl.pallas_call(
        _hyena_prep_kernel,
        grid=(nb, n // tm),
        in_specs=in_specs,
        out_specs=(pl.BlockSpec((1, tm, D_HYENA), lambda b, i: (b, i, 0)),
                   pl.BlockSpec((1, tm, D_HYENA), lambda b, i: (b, i, 0))),
        out_shape=(jax.ShapeDtypeStruct((nb, n, D_HYENA), BF16),
                   jax.ShapeDtypeStruct((nb, n, D_HYENA), BF16)),
        compiler_params=_cparams("parallel", "parallel"),
        name="hyena_prep",
    )(z, z, z, conv_w, conv_b.reshape(1, c3))


def _filter_kernel(tc_ref, tr_ref, om_ref, band_ref, sgn_ref, ph_ref, w1_ref, b1_ref, wi_ref, bi_ref,
                   fr_ref, wo_ref, dl_ref, h_ref):
    tr = tr_ref[...]
    rows, tl = FILTER_EMB_ROWS, tr.shape[1]
    r = lax.broadcasted_iota(jnp.int32, (rows, tl), 0)
    trig = jnp.sin(sgn_ref[...] * (band_ref[...] * om_ref[...]) + ph_ref[...])
    emb = jnp.where(r == 0, tr, jnp.where(r <= 2 * FILTER_EMB_BANDS, trig, 0.0))
    fr = fr_ref[...]
    h = jnp.sin(fr * (_dot3(w1_ref[...], emb) + b1_ref[...]))
    for i in range(FILTER_INNER):
        h = jnp.sin(fr * (_dot3(wi_ref[i], h) + bi_ref[i]))
    h = _dot3(h.T, wo_ref[...]) * jnp.exp(-tc_ref[...] * dl_ref[...])
    c = D_HYENA
    h_ref[0] = h[:, :c]
    first = (pl.program_id(0) == 0) & (lax.broadcasted_iota(jnp.int32, (tl, c), 0) == 0)
    h_ref[1] = jnp.where(first, 0.0, h[:, c:])


def _hyena_filter(n, w1, b1, w_inner, b_inner, freq, w_out):
    tl = min(FILT_TL, n)
    t = jnp.linspace(0.0, 1.0, n, dtype=F32)
    omega = ((2.0 * math.pi / n) * jnp.arange(n, dtype=F32))[None, :]
    nbands, rows, fh = FILTER_EMB_BANDS, FILTER_EMB_ROWS, FILTER_HIDDEN
    bands = jnp.linspace(1e-4, nbands - 1, nbands, dtype=F32)
    pad = jnp.zeros((rows - 1 - 2 * nbands,), F32)
    zero1, ones, zeros = jnp.zeros((1,), F32), jnp.ones((nbands,), F32), jnp.zeros((nbands,), F32)
    band_col = jnp.concatenate([zero1, bands, bands, pad])[:, None]
    sgn_col = jnp.concatenate([zero1, ones, -ones, pad])[:, None]
    ph_col = jnp.concatenate([zero1, ones * (0.5 * math.pi), zeros, pad])[:, None]
    w1t = jnp.concatenate([w1.astype(F32), jnp.zeros((rows - w1.shape[0], fh), F32)], axis=0).T
    min_decay = math.log(DECAY_TARGET) / SLOW_DECAY_PCT
    max_decay = math.log(DECAY_TARGET) / FAST_DECAY_PCT
    deltas = jnp.abs(jnp.linspace(min_decay, max_decay, D_HYENA, dtype=F32))
    dl = jnp.concatenate([deltas, deltas])[None, :]
    args = [t[:, None], t[None, :], omega, band_col, sgn_col, ph_col, w1t, b1.reshape(fh, 1),
            jnp.swapaxes(w_inner, 1, 2), b_inner.reshape(FILTER_INNER, fh, 1), freq.reshape(fh, 1), w_out, dl]

    def cs(a):
        nd = a.ndim
        return pl.BlockSpec(a.shape, lambda i: (0,) * nd)

    in_specs = [pl.BlockSpec((tl, 1), lambda i: (i, 0)), pl.BlockSpec((1, tl), lambda i: (0, i)),
                pl.BlockSpec((1, tl), lambda i: (0, i))]
    in_specs += [cs(a) for a in args[3:]]
    return pl.pallas_call(
        _filter_kernel,
        grid=(n // tl,),
        in_specs=in_specs,
        out_specs=pl.BlockSpec((2, tl, D_HYENA), lambda i: (0, i, 0)),
        out_shape=jax.ShapeDtypeStruct((2, n, D_HYENA), F32),
        compiler_params=_cparams("parallel"),
        name="hyena_filter",
    )(*args)


def _dft_tables(n):
    m2 = FFT_M2
    m1 = 2 * n // m2
    m = 2 * n
    k1 = np.arange(m1, dtype=np.float64)[:, None]
    n1 = np.arange(m1 // 2, dtype=np.float64)[None, :]
    ang1 = 2.0 * np.pi * k1 * n1 / m1
    f1 = np.concatenate([np.cos(ang1), -np.sin(ang1)], axis=0)
    finv = np.concatenate([np.cos(ang1).T, -np.sin(ang1).T], axis=1)
    a = np.arange(m2, dtype=np.float64)
    ang2 = 2.0 * np.pi * a[:, None] * a[None, :] / m2
    angt = 2.0 * np.pi * k1 * a[None, :] / m
    return dict(
        m1=m1,
        f1=jnp.asarray(f1, F32), finv=jnp.asarray(finv, F32),
        f2r=jnp.asarray(np.cos(ang2), F32), f2i=jnp.asarray(-np.sin(ang2), F32),
        twr=jnp.asarray(np.cos(angt)[:, None, :], F32), twi=jnp.asarray(-np.sin(angt)[:, None, :], F32))


def _dft_a_kernel(x_ref, f_ref, o_ref):
    o_ref[0] = _dot(f_ref[...].astype(BF16), x_ref[0].astype(BF16)).astype(BF16)


def _dft_stage_a(x, f1):
    g, r, w = x.shape
    tn = min(FFT_TN, w)
    rows = f1.shape[0]
    return pl.pallas_call(
        _dft_a_kernel,
        grid=(g, w // tn),
        in_specs=[pl.BlockSpec((1, r, tn), lambda b, j: (b, 0, j)),
                  pl.BlockSpec(f1.shape, lambda b, j: (0, 0))],
        out_specs=pl.BlockSpec((1, rows, tn), lambda b, j: (b, 0, j)),
        out_shape=jax.ShapeDtypeStruct((g, rows, w), BF16),
        compiler_params=_cparams("parallel", "parallel"),
        name="dft_stage_a",
    )(x, f1)


def _twiddled_dft(f2r, f2i, twr, twi):
    gr = f2r * twr - f2i * twi
    gi = f2r * twi + f2i * twr
    return jnp.concatenate([jnp.concatenate([gr, -gi], axis=1),
                            jnp.concatenate([gi, gr], axis=1)], axis=0)


def _filter_spectrum_kernel(a_ref, f2r_ref, f2i_ref, twr_ref, twi_ref, k_ref, *, kc):
    m2 = FFT_M2
    for j in range(kc):
        g = _twiddled_dft(f2r_ref[...], f2i_ref[...], twr_ref[j], twi_ref[j]).astype(BF16)
        xf = _dot(g, a_ref[0, :, j].reshape(2 * m2, -1))
        xb = _dot(g, a_ref[1, :, j].reshape(2 * m2, -1))
        k_ref[0, j] = xf[:m2] + xb[:m2]
        k_ref[1, j] = xf[m2:] - xb[m2:]


def _filter_spectrum(a5, tabs):
    _, _, m1, m2, c = a5.shape
    kc = FFT_KC
    in_specs = [pl.BlockSpec((2, 2, kc, m2, c), lambda i: (0, 0, i, 0, 0)),
                pl.BlockSpec((m2, m2), lambda i: (0, 0)), pl.BlockSpec((m2, m2), lambda i: (0, 0)),
                pl.BlockSpec((kc, 1, m2), lambda i: (i, 0, 0)), pl.BlockSpec((kc, 1, m2), lambda i: (i, 0, 0))]
    return pl.pallas_call(
        functools.partial(_filter_spectrum_kernel, kc=kc),
        grid=(m1 // kc,),
        in_specs=in_specs,
        out_specs=pl.BlockSpec((2, kc, m2, c), lambda i: (0, i, 0, 0)),
        out_shape=jax.ShapeDtypeStruct((2, m1, m2, c), F32),
        compiler_params=_cparams("parallel"),
        name="filter_spectrum",
    )(a5, tabs["f2r"], tabs["f2i"], tabs["twr"], tabs["twi"])


def _spectral_kernel(a_ref, kf_ref, f2r_ref, f2i_ref, twr_ref, twi_ref, b_ref, *, kc, nb):
    m2 = FFT_M2
    for j in range(kc):
        g32 = _twiddled_dft(f2r_ref[...], f2i_ref[...], twr_ref[j], twi_ref[j])
        g = g32.astype(BF16)
        gt = g32.T.astype(BF16)
        kr = kf_ref[0, j]
        ki = kf_ref[1, j]
        for b in range(nb):
            x = _dot(g, a_ref[b, :, j].reshape(2 * m2, -1))
            xr, xi = x[:m2], x[m2:]
            y = jnp.concatenate([xr * kr - xi * ki, xr * ki + xi * kr], axis=0).astype(BF16)
            z = _dot(gt, y)
            b_ref[b, :, j] = z.reshape(2, m2, -1).astype(BF16)


def _spectral_multiply(a5, kf, tabs):
    nb, _, m1, m2, c = a5.shape
    kc = FFT_KC
    in_specs = [pl.BlockSpec((nb, 2, kc, m2, c), lambda i: (0, 0, i, 0, 0)),
                pl.BlockSpec((2, kc, m2, c), lambda i: (0, i, 0, 0)),
                pl.BlockSpec((m2, m2), lambda i: (0, 0)), pl.BlockSpec((m2, m2), lambda i: (0, 0)),
                pl.BlockSpec((kc, 1, m2), lambda i: (i, 0, 0)), pl.BlockSpec((kc, 1, m2), lambda i: (i, 0, 0))]
    return pl.pallas_call(
        functools.partial(_spectral_kernel, kc=kc, nb=nb),
        grid=(m1 // kc,),
        in_specs=in_specs,
        out_specs=pl.BlockSpec((nb, 2, kc, m2, c), lambda i: (0, 0, i, 0, 0)),
        out_shape=jax.ShapeDtypeStruct((nb, 2, m1, m2, c), BF16),
        compiler_params=_cparams("parallel"),
        name="spectral_multiply",
    )(a5, kf, tabs["f2r"], tabs["f2i"], tabs["twr"], tabs["twi"])


def _idft_a_kernel(b_ref, f_ref, x0_ref, u_ref, bias_ref, y_ref, *, inv_m):
    conv = _dot(f_ref[...].astype(BF16), b_ref[0]) * inv_m
    u = u_ref[0].astype(F32)
    y_ref[0] = (x0_ref[0].astype(F32) * (conv + u * bias_ref[...])).astype(BF16)


def _idft_stage_a(bm, finv, x0v, uv, bias_row, inv_m):
    g, rows, w = bm.shape
    r = finv.shape[0]
    tn = min(FFT_TN, w)
    return pl.pallas_call(
        functools.partial(_idft_a_kernel, inv_m=inv_m),
        grid=(g, w // tn),
        in_specs=[pl.BlockSpec((1, rows, tn), lambda b, j: (b, 0, j)),
                  pl.BlockSpec(finv.shape, lambda b, j: (0, 0)),
                  pl.BlockSpec((1, r, tn), lambda b, j: (b, 0, j)),
                  pl.BlockSpec((1, r, tn), lambda b, j: (b, 0, j)),
                  pl.BlockSpec((1, tn), lambda b, j: (0, 0))],
        out_specs=pl.BlockSpec((1, r, tn), lambda b, j: (b, 0, j)),
        out_shape=jax.ShapeDtypeStruct((g, r, w), BF16),
        compiler_params=_cparams("parallel", "parallel"),
        name="idft_stage_a",
    )(bm, finv, x0v, uv, bias_row)


def _hyena_mixer(z, conv_w, conv_b, w1, b1, w_inner, b_inner, freq, w_out, bias):
    nb, n, _ = z.shape
    c = D_HYENA
    tabs = _dft_tables(n)
    m1, m2 = tabs["m1"], FFT_M2
    w = m2 * c
    u, x0 = _hyena_prep(z, conv_w, conv_b)
    h = _hyena_filter(n, w1, b1, w_inner, b_inner, freq, w_out)
    ah = _dft_stage_a(h.reshape(2, m1 // 2, w), tabs["f1"])
    kf = _filter_spectrum(ah.reshape(2, 2, m1, m2, c), tabs)
    au = _dft_stage_a(u.reshape(nb, m1 // 2, w), tabs["f1"])
    bm = _spectral_multiply(au.reshape(nb, 2, m1, m2, c), kf, tabs)
    tn = min(FFT_TN, w)
    bias_row = jnp.tile(bias.astype(F32), tn // c)[None, :]
    y = _idft_stage_a(bm.reshape(nb, 2 * m1, w), tabs["finv"], x0.reshape(nb, m1 // 2, w),
                      u.reshape(nb, m1 // 2, w), bias_row, 1.0 / (2 * n))
    return y.reshape(nb, n, c)


def _rope_tables(n):
    rows = n // GRID_W
    row = jnp.broadcast_to(jnp.arange(rows, dtype=F32)[:, None], (rows, GRID_W)).reshape(-1)
    col = jnp.broadcast_to(jnp.arange(GRID_W, dtype=F32)[None, :], (rows, GRID_W)).reshape(-1)
    axis_dim = DIFF_HEAD_DIM // 2
    inv_freq = ROPE_BASE ** (-jnp.arange(0, axis_dim, 2, dtype=F32) / axis_dim)
    ang_r = row[:, None] * inv_freq
    ang_c = col[:, None] * inv_freq
    cos64 = jnp.concatenate([jnp.cos(ang_r), jnp.cos(ang_r), jnp.cos(ang_c), jnp.cos(ang_c)], axis=1)
    sin64 = jnp.concatenate([-jnp.sin(ang_r), jnp.sin(ang_r), -jnp.sin(ang_c), jnp.sin(ang_c)], axis=1)
    reps = LANES // DIFF_HEAD_DIM
    return (jnp.tile(cos64, (1, reps)), jnp.tile(sin64, (1, reps)), cos64.T, sin64.T)


def kernel(x, c, ctx, c_ctx, ada_w, ada_b, norm_g, ffn_w_gate, ffn_w_up, ffn_w_down, w_in, w_out,
           hyena_conv_w, hyena_conv_b, filt_w1, filt_b1, filt_w_inner, filt_b_inner, filt_sin_freq,
           filt_w_out, hyena_bias, diff_lambda, diff_subln_g, final_g):
    nb, n, d = x.shape
    layer = 0
    hy_end = HYENA_N_PROJ * D_HYENA
    q_end = hy_end + D_DIFF
    k_end = q_end + D_DIFF

    cc = jnp.concatenate([c, c_ctx[None, :], jnp.zeros((8 - nb - 1, d), F32)], axis=0)
    mod = _modulation(cc, ada_w[layer], ada_b[layer])
    mod_lat = [mod[:nb, j * d:(j + 1) * d][:, None, :] for j in range(N_ADA)]
    mod_ctx = [mod[nb:nb + 1, j * d:(j + 1) * d][:, None, :] for j in range(N_ADA)]

    def ffn_weights(i):
        return (ffn_w_gate[layer, i].astype(BF16), ffn_w_up[layer, i].astype(BF16),
                ffn_w_down[layer, i].astype(BF16))

    ffn0 = ffn_weights(0)
    h_lat = _ffn_half_step(x, norm_g[layer, 0], mod_lat[0], mod_lat[1], mod_lat[2], *ffn0)
    h_ctx = _ffn_half_step(ctx, norm_g[layer, 0], mod_ctx[0], mod_ctx[1], mod_ctx[2], *ffn0)

    w = w_in[layer]
    wzk = jnp.concatenate([w[:, :hy_end], w[:, q_end:k_end]], axis=1).astype(BF16)
    wqv = jnp.concatenate([w[:, hy_end:q_end], w[:, k_end:]], axis=1).T.astype(BF16)
    ck, sk, cq, sq = _rope_tables(n)
    z, k_lat, qT, vT4 = _input_projection(h_lat, norm_g[layer, 1], mod_lat[3], mod_lat[4],
                                          wzk, wqv, ck, sk, cq, sq)
    k_ctx, vT_ctx = _context_projection(h_ctx, norm_g[layer, 1], mod_ctx[3], mod_ctx[4],
                                        w[:, q_end:k_end].astype(BF16), w[:, k_end:].T.astype(BF16))

    y_diff = _diff_attention(qT, k_lat, vT4, k_ctx, vT_ctx, diff_lambda[layer].astype(F32),
                             diff_subln_g[layer])
    y_hy = _hyena_mixer(z, hyena_conv_w[layer], hyena_conv_b[layer], filt_w1[layer], filt_b1[layer],
                        filt_w_inner[layer], filt_b_inner[layer], filt_sin_freq[layer],
                        filt_w_out[layer], hyena_bias[layer])

    wo = w_out[layer].astype(BF16)
    mixer = (y_hy, y_diff, wo[:D_HYENA], wo[D_HYENA:], mod_lat[5])
    return _ffn_half_step(h_lat, norm_g[layer, 2], mod_lat[6], mod_lat[7], mod_lat[8],
                          *ffn_weights(1), mixer=mixer, final_g=final_g)
```

```python
import functools
import math

import numpy as np
import jax
import jax.numpy as jnp
from jax import lax
from jax.experimental import pallas as pl
from jax.experimental.pallas import tpu as pltpu

F32 = jnp.float32
BF16 = jnp.bfloat16

N_ADA = 9
FFN_HIDDEN = 2816
D_HYENA = 512
HYENA_N_PROJ = 3
FILTER_EMB_BANDS = 16
FILTER_HIDDEN = 64
FILTER_EMB_ROWS = 40
FILTER_INNER = 2
DECAY_TARGET = 1e-2
FAST_DECAY_PCT = 0.3
SLOW_DECAY_PCT = 1.5
N_DIFF_HEADS = 4
DIFF_HEAD_DIM = 64
D_DIFF = N_DIFF_HEADS * 2 * DIFF_HEAD_DIM
GRID_W = 64
ROPE_BASE = 10000.0
RMS_EPS = 1e-6
SUBLN_EPS = 1e-5
LAM_INIT = 0.8 - 0.6 * math.exp(-0.3 * 0)

LANES = 128
BF16_ROWS = 16
V7X_VMEM_BYTES = 64 * 2**20
VMEM_LIMIT = 56 * 2**20

FFN_TM = 512
FFN_TF = 256
PROJ_TM = 512
ATT_TK = 512
ATT_STEPS = 2
ATT_TQ = 2048
FFT_M2 = 128
FFT_TN = 4096
FFT_KC = 8
FILT_TL = 1024
NEG_BIG = -1e30

LOG2E = 1.4426950408889634


def _cparams(*sem):
    return pltpu.CompilerParams(dimension_semantics=sem, vmem_limit_bytes=VMEM_LIMIT)


def _dot(a, b):
    return jnp.dot(a, b, preferred_element_type=F32)


def _dot3(a, b):
    a_hi = a.astype(BF16)
    a_lo = (a - a_hi.astype(F32)).astype(BF16)
    b_hi = b.astype(BF16)
    b_lo = (b - b_hi.astype(F32)).astype(BF16)
    return _dot(a_hi, b_hi) + (_dot(a_lo, b_hi) + _dot(a_hi, b_lo))


def _silu(x):
    return x * (1.0 / (1.0 + jnp.exp(-x)))


def _adaln(s, g, shift, scale):
    ms = jnp.mean(s * s, axis=-1, keepdims=True)
    return (s * lax.rsqrt(ms + RMS_EPS) * g) * (1.0 + scale) + shift


def _mod_kernel(c_ref, w_ref, b_ref, o_ref):
    o_ref[...] = _dot3(_silu(c_ref[...]), w_ref[...]) + b_ref[...]


def _modulation(cc, ada_w, ada_b):
    rows, d = cc.shape
    n = ada_w.shape[1]
    tn = 768
    return pl.pallas_call(
        _mod_kernel,
        grid=(n // tn,),
        in_specs=[pl.BlockSpec((rows, d), lambda j: (0, 0)),
                  pl.BlockSpec((d, tn), lambda j: (0, j)),
                  pl.BlockSpec((1, tn), lambda j: (0, j))],
        out_specs=pl.BlockSpec((rows, tn), lambda j: (0, j)),
        out_shape=jax.ShapeDtypeStruct((rows, n), F32),
        compiler_params=_cparams("parallel"),
        name="modulation",
    )(cc, ada_w, ada_b.reshape(1, n))


def _mod_spec(arr, nb):
    d = arr.shape[-1]
    if arr.shape[0] == nb:
        return pl.BlockSpec((1, 1, d), lambda b, i: (b, 0, 0))
    return pl.BlockSpec((1, 1, d), lambda b, i: (0, 0, 0))


def _const_spec(arr):
    nd = arr.ndim
    return pl.BlockSpec(arr.shape, lambda b, i: (0,) * nd)


def _ffn_kernel(*refs, n_chunks, tf, mix, final):
    refs = list(refs)
    o_ref = refs.pop()
    s_ref, g_ref, sh_ref, sc_ref, gt_ref, wg_ref, wu_ref, wd_ref = refs[:8]
    rest = refs[8:]
    s = s_ref[0]
    if mix:
        yh_ref, yd_ref, woa_ref, wob_ref, gm_ref = rest[:5]
        rest = rest[5:]
        s = s + gm_ref[0] * (_dot(yh_ref[0], woa_ref[...]) + _dot(yd_ref[0], wob_ref[...]))
    ub = _adaln(s, g_ref[...], sh_ref[0], sc_ref[0]).astype(BF16)
    acc = jnp.zeros(s.shape, F32)
    for f in range(n_chunks):
        cols = slice(f * tf, (f + 1) * tf)
        gate = _dot(ub, wg_ref[:, cols])
        up = _dot(ub, wu_ref[:, cols])
        a = (_silu(gate) * up).astype(BF16)
        acc = acc + _dot(a, wd_ref[cols, :])
    out = s + (0.5 * gt_ref[0]) * acc
    if final:
        ms = jnp.mean(out * out, axis=-1, keepdims=True)
        out = out * lax.rsqrt(ms + RMS_EPS) * rest[0][...]
    o_ref[0] = out


def _ffn_half_step(s, g, shift, scale, gate, wg, wu, wd, mixer=None, final_g=None):
    nb, t, d = s.shape
    tm = min(FFN_TM, t)
    args = [s, g.reshape(1, d), shift, scale, gate, wg, wu, wd]
    in_specs = [pl.BlockSpec((1, tm, d), lambda b, i: (b, i, 0)),
                _const_spec(args[1]),
                _mod_spec(shift, nb), _mod_spec(scale, nb), _mod_spec(gate, nb),
                _const_spec(wg), _const_spec(wu), _const_spec(wd)]
    if mixer is not None:
        yh, yd, woa, wob, gm = mixer
        args += [yh, yd, woa, wob, gm]
        in_specs += [pl.BlockSpec((1, tm, yh.shape[2]), lambda b, i: (b, i, 0)),
                     pl.BlockSpec((1, tm, yd.shape[2]), lambda b, i: (b, i, 0)),
                     _const_spec(woa), _const_spec(wob), _mod_spec(gm, nb)]
    if final_g is not None:
        args.append(final_g.reshape(1, d))
        in_specs.append(_const_spec(args[-1]))
    kern = functools.partial(_ffn_kernel, n_chunks=wg.shape[1] // FFN_TF, tf=FFN_TF,
                             mix=mixer is not None, final=final_g is not None)
    return pl.pallas_call(
        kern,
        grid=(nb, t // tm),
        in_specs=in_specs,
        out_specs=pl.BlockSpec((1, tm, d), lambda b, i: (b, i, 0)),
        out_shape=jax.ShapeDtypeStruct((nb, t, d), F32),
        compiler_params=_cparams("parallel", "parallel"),
        name="ffn_mix_final" if mixer is not None else "ffn",
    )(*args)


def _rope_partner(x, axis):
    n = x.shape[axis]
    idx = lax.broadcasted_iota(jnp.int32, x.shape, axis)
    first_half = (idx & 16) == 0
    return jnp.where(first_half, pltpu.roll(x, n - 16, axis), pltpu.roll(x, 16, axis))


def _inproj_kernel(s_ref, sp_ref, sn_ref, g_ref, sh_ref, sc_ref, wzk_ref, wqv_ref, ck_ref, sk_ref,
                   cq_ref, sq_ref, cw_ref, cb_ref, u_ref, x0_ref, k_ref, qT_ref, vT_ref, *, q_scale):
    i = pl.program_id(1)
    last = pl.num_programs(1) - 1
    tm = s_ref.shape[1]
    rows = jnp.concatenate([s_ref[0], sp_ref[0], sn_ref[0]], axis=0)
    ub_ext = _adaln(rows, g_ref[...], sh_ref[0], sc_ref[0]).astype(BF16)
    ub = ub_ext[:tm]
    zk = _dot(ub_ext, wzk_ref[...])
    hy = HYENA_N_PROJ * D_HYENA
    z = zk[:tm, :hy]
    prev = jnp.where(i == 0, 0.0, zk[tm + 7:tm + 8, :hy])
    nxt = jnp.where(i == last, 0.0, zk[tm + 8:tm + 9, :hy])
    row = lax.broadcasted_iota(jnp.int32, z.shape, 0)
    z_dn = jnp.where(row == 0, prev, pltpu.roll(z, 1, 0))
    z_up = jnp.where(row == tm - 1, nxt, pltpu.roll(z, tm - 1, 0))
    y = z_dn * cw_ref[0:1] + z * cw_ref[1:2] + z_up * cw_ref[2:3] + cb_ref[...]
    c = D_HYENA
    x0_ref[0] = y[:, :c].astype(BF16)
    u_ref[0] = (y[:, c:2 * c] * y[:, 2 * c:]).astype(BF16)
    k = zk[:tm, hy:]
    reps = D_DIFF // LANES
    ck = jnp.concatenate([ck_ref[...]] * reps, axis=1)
    sk = jnp.concatenate([sk_ref[...]] * reps, axis=1)
    k_ref[0] = (k * ck + _rope_partner(k, 1) * sk).astype(BF16)
    qv = lax.dot_general(wqv_ref[...], ub, (((1,), (1,)), ((), ())), preferred_element_type=F32)
    q = qv[:D_DIFF]
    groups = D_DIFF // DIFF_HEAD_DIM
    cq = jnp.broadcast_to(cq_ref[...][None], (groups, DIFF_HEAD_DIM, tm)).reshape(D_DIFF, tm)
    sq = jnp.broadcast_to(sq_ref[...][None], (groups, DIFF_HEAD_DIM, tm)).reshape(D_DIFF, tm)
    q = (q * cq + _rope_partner(q, 0) * sq) * q_scale
    qT_ref[0] = q.astype(BF16)
    v = qv[D_DIFF:].astype(BF16)
    tk = vT_ref.shape[3]
    for j in range(vT_ref.shape[1]):
        vT_ref[0, j] = v[:, j * tk:(j + 1) * tk]


def _input_projection(h, g, shift, scale, wzk, wqv, ck, sk, cq, sq, conv_w, conv_b):
    nb, n, d = h.shape
    tm = PROJ_TM
    r, nblk8 = tm // 8, n // 8
    c3 = conv_w.shape[1]
    q_scale = DIFF_HEAD_DIM ** -0.5 * LOG2E
    args = [h, h, h, g.reshape(1, d), shift, scale, wzk, wqv, ck, sk, cq, sq, conv_w, conv_b.reshape(1, c3)]
    in_specs = [pl.BlockSpec((1, tm, d), lambda b, i: (b, i, 0)),
                pl.BlockSpec((1, 8, d), lambda b, i: (b, jnp.maximum(i * r - 1, 0), 0)),
                pl.BlockSpec((1, 8, d), lambda b, i: (b, jnp.minimum((i + 1) * r, nblk8 - 1), 0)),
                _const_spec(args[3]), _mod_spec(shift, nb), _mod_spec(scale, nb),
                _const_spec(wzk), _const_spec(wqv),
                pl.BlockSpec((tm, LANES), lambda b, i: (i, 0)),
                pl.BlockSpec((tm, LANES), lambda b, i: (i, 0)),
                pl.BlockSpec((DIFF_HEAD_DIM, tm), lambda b, i: (0, i)),
                pl.BlockSpec((DIFF_HEAD_DIM, tm), lambda b, i: (0, i)),
                _const_spec(conv_w), _const_spec(args[-1])]
    out_shape = (jax.ShapeDtypeStruct((nb, n, D_HYENA), BF16),
                 jax.ShapeDtypeStruct((nb, n, D_HYENA), BF16),
                 jax.ShapeDtypeStruct((nb, n, D_DIFF), BF16),
                 jax.ShapeDtypeStruct((nb, D_DIFF, n), BF16),
                 jax.ShapeDtypeStruct((nb, n // ATT_TK, D_DIFF, ATT_TK), BF16))
    out_specs = (pl.BlockSpec((1, tm, D_HYENA), lambda b, i: (b, i, 0)),
                 pl.BlockSpec((1, tm, D_HYENA), lambda b, i: (b, i, 0)),
                 pl.BlockSpec((1, tm, D_DIFF), lambda b, i: (b, i, 0)),
                 pl.BlockSpec((1, D_DIFF, tm), lambda b, i: (b, 0, i)),
                 pl.BlockSpec((1, tm // ATT_TK, D_DIFF, ATT_TK), lambda b, i: (b, i, 0, 0)))
    return pl.pallas_call(
        functools.partial(_inproj_kernel, q_scale=q_scale),
        grid=(nb, n // tm),
        in_specs=in_specs, out_specs=out_specs, out_shape=out_shape,
        compiler_params=_cparams("parallel", "parallel"),
        name="input_projection",
    )(*args)


def _inproj_ctx_kernel(s_ref, g_ref, sh_ref, sc_ref, wk_ref, wv_ref, k_ref, vT_ref):
    ub = _adaln(s_ref[0], g_ref[...], sh_ref[0], sc_ref[0]).astype(BF16)
    k_ref[0] = _dot(ub, wk_ref[...]).astype(BF16)
    vT_ref[0] = lax.dot_general(wv_ref[...], ub, (((1,), (1,)), ((), ())),
                                preferred_element_type=F32).astype(BF16)


def _context_projection(h, g, shift, scale, wk, wvT):
    nb, c, d = h.shape
    args = [h, g.reshape(1, d), shift, scale, wk, wvT]
    in_specs = [pl.BlockSpec((1, c, d), lambda b, i: (b, 0, 0)),
                _const_spec(args[1]), _mod_spec(shift, nb), _mod_spec(scale, nb),
                _const_spec(wk), _const_spec(wvT)]
    return pl.pallas_call(
        _inproj_ctx_kernel,
        grid=(nb, 1),
        in_specs=in_specs,
        out_specs=(pl.BlockSpec((1, c, D_DIFF), lambda b, i: (b, 0, 0)),
                   pl.BlockSpec((1, D_DIFF, c), lambda b, i: (b, 0, 0))),
        out_shape=(jax.ShapeDtypeStruct((nb, c, D_DIFF), BF16),
                   jax.ShapeDtypeStruct((nb, D_DIFF, c), BF16)),
        compiler_params=_cparams("parallel", "parallel"),
        name="context_projection",
    )(*args)


def _attn_kernel(qT_ref, k_ref, vT_ref, kc_ref, vTc_ref, lp_ref, sg_ref, o_ref,
                 acc0_ref, acc1_ref, qa_ref, qb_ref, *sp_refs, n_chunks, tk):
    d = DIFF_HEAD_DIM
    s_refs = (sp_refs[0:2], sp_refs[2:4])
    p_refs = (sp_refs[4:6], sp_refs[6:8])
    q = qT_ref[0]
    tq = q.shape[1]
    row = lax.broadcasted_iota(jnp.int32, q.shape, 0)
    zero = jnp.zeros_like(q)
    qa_ref[...] = jnp.where(row < d, q, zero)
    qb_ref[...] = jnp.where(row < d, zero, q)
    acc0_ref[...] = jnp.zeros_like(acc0_ref)
    acc1_ref[...] = jnp.zeros_like(acc1_ref)
    q_refs = (qa_ref, qb_ref)
    acc_refs = (acc0_ref, acc1_ref)

    def with_ones(vc):
        r = lax.broadcasted_iota(jnp.int32, (BF16_ROWS, vc.shape[1]), 0)
        return jnp.concatenate([vc, jnp.where(r == 0, 1.0, 0.0).astype(BF16)], axis=0)

    def scores(kc, slot):
        cmax = []
        for h in range(2):
            s = _dot(kc, q_refs[h][...])
            s_refs[slot][h][...] = s
            cmax.append(jnp.max(s, axis=0, keepdims=True))
        return tuple(cmax)

    def probs(s, cmax, m):
        m_new = jnp.maximum(m, cmax)
        return jnp.exp2((s - m_new).astype(BF16)), m_new, jnp.exp2(m - m_new)

    def softmax_stage(slot, cmax, stats):
        out = []
        for h in range(2):
            p, m, alpha = probs(s_refs[slot][h][...], cmax[h], stats[h][0])
            p_refs[slot][h][...] = p
            out.append((m, alpha))
        return tuple(out)

    def values(vc, slot, stats):
        for h in range(2):
            acc_refs[h][...] = stats[h][1] * acc_refs[h][...] + _dot(vc, p_refs[slot][h][...])

    def lat_keys(t):
        return k_ref[0, pl.ds(pl.multiple_of(t * tk, tk), tk), :]

    def step(t, slot, cmax_other, stats, with_scores=True):
        cmax = scores(lat_keys(t + 2), slot) if with_scores else None
        values(with_ones(vT_ref[0, t]), slot, stats)
        new_stats = softmax_stage(1 - slot, cmax_other, stats)
        return cmax, new_stats

    neg = jnp.full((1, tq), NEG_BIG, F32)
    one = jnp.ones((1, tq), F32)
    cmax0 = scores(lat_keys(0), 0)
    cmax1 = scores(lat_keys(1), 1)
    carry = (cmax1, softmax_stage(0, cmax0, ((neg, one), (neg, one))))

    def body(i, carry):
        for j in range(ATT_STEPS):
            carry = step(ATT_STEPS * i + j, j % 2, *carry)
        return carry

    n_loop = (n_chunks - 2) // ATT_STEPS
    carry = lax.fori_loop(0, n_loop, body, carry)
    for t in range(n_loop * ATT_STEPS, n_chunks - 1):
        carry = step(t, t % 2, *carry, with_scores=t + 2 < n_chunks)
    stats = carry[1]
    values(with_ones(vT_ref[0, n_chunks - 1]), (n_chunks - 1) % 2, stats)

    kc = kc_ref[0]
    vc = with_ones(vTc_ref[0])
    for h in range(2):
        s = _dot(kc, q_refs[h][...])
        p, _, alpha = probs(s, jnp.max(s, axis=0, keepdims=True), stats[h][0])
        acc_refs[h][...] = alpha * acc_refs[h][...] + _dot(vc, p)

    lp = lp_ref[...]
    lam = (jnp.exp(jnp.sum(lp[0:1] * lp[1:2], axis=1, keepdims=True))
           - jnp.exp(jnp.sum(lp[2:3] * lp[3:4], axis=1, keepdims=True)) + LAM_INIT)
    hd = 2 * d
    o0 = acc0_ref[:hd, :] / acc0_ref[hd:hd + 1, :]
    o1 = acc1_ref[:hd, :] / acc1_ref[hd:hd + 1, :]
    oT = o0 - lam * o1
    ms = jnp.mean(oT * oT, axis=0, keepdims=True)
    oT = oT * lax.rsqrt(ms + SUBLN_EPS)
    o_ref[0] = ((oT.T * sg_ref[...]) * (1.0 - LAM_INIT)).astype(BF16)


def _diff_attention(qT, k, vT4, k_ctx, vT_ctx, lam_params, subln_g):
    nb, n, _ = k.shape
    hd = 2 * DIFF_HEAD_DIM
    n_chunks, tk = vT4.shape[1], vT4.shape[3]
    c = k_ctx.shape[1]
    tq = ATT_TQ
    in_specs = [pl.BlockSpec((1, hd, tq), lambda b, h, i: (b, h, i)),
                pl.BlockSpec((1, n, hd), lambda b, h, i: (b, 0, h)),
                pl.BlockSpec((1, n_chunks, hd, tk), lambda b, h, i: (b, 0, h, 0)),
                pl.BlockSpec((1, c, hd), lambda b, h, i: (b, 0, h)),
                pl.BlockSpec((1, hd, c), lambda b, h, i: (b, h, 0)),
                pl.BlockSpec(lam_params.shape, lambda b, h, i: (0, 0)),
                pl.BlockSpec((1, hd), lambda b, h, i: (0, 0))]
    return pl.pallas_call(
        functools.partial(_attn_kernel, n_chunks=n_chunks, tk=tk),
        grid=(nb, N_DIFF_HEADS, n // tq),
        in_specs=in_specs,
        out_specs=pl.BlockSpec((1, tq, hd), lambda b, h, i: (b, i, h)),
        out_shape=jax.ShapeDtypeStruct((nb, n, D_DIFF), BF16),
        scratch_shapes=([pltpu.VMEM((hd + BF16_ROWS, tq), F32)] * 2 + [pltpu.VMEM((hd, tq), BF16)] * 2
                        + [pltpu.VMEM((tk, tq), F32)] * 4 + [pltpu.VMEM((tk, tq), BF16)] * 4),
        compiler_params=_cparams("parallel", "parallel", "arbitrary"),
        name="diff_attention",
    )(qT, k, vT4, k_ctx, vT_ctx, lam_params, subln_g.reshape(1, hd))


def _filter_kernel(tc_ref, tr_ref, om_ref, band_ref, sgn_ref, ph_ref, w1_ref, b1_ref, wi_ref, bi_ref,
                   fr_ref, wo_ref, dl_ref, h_ref):
    tr = tr_ref[...]
    rows, tl = FILTER_EMB_ROWS, tr.shape[1]
    r = lax.broadcasted_iota(jnp.int32, (rows, tl), 0)
    trig = jnp.sin(sgn_ref[...] * (band_ref[...] * om_ref[...]) + ph_ref[...])
    emb = jnp.where(r == 0, tr, jnp.where(r <= 2 * FILTER_EMB_BANDS, trig, 0.0))
    fr = fr_ref[...]
    h = jnp.sin(fr * (_dot3(w1_ref[...], emb) + b1_ref[...]))
    for i in range(FILTER_INNER):
        h = jnp.sin(fr * (_dot3(wi_ref[i], h) + bi_ref[i]))
    h = _dot3(h.T, wo_ref[...]) * jnp.exp(-tc_ref[...] * dl_ref[...])
    c = D_HYENA
    h_ref[0] = h[:, :c]
    first = (pl.program_id(0) == 0) & (lax.broadcasted_iota(jnp.int32, (tl, c), 0) == 0)
    h_ref[1] = jnp.where(first, 0.0, h[:, c:])


def _hyena_filter(n, w1, b1, w_inner, b_inner, freq, w_out):
    tl = min(FILT_TL, n)
    t = jnp.linspace(0.0, 1.0, n, dtype=F32)
    omega = ((2.0 * math.pi / n) * jnp.arange(n, dtype=F32))[None, :]
    nbands, rows, fh = FILTER_EMB_BANDS, FILTER_EMB_ROWS, FILTER_HIDDEN
    bands = jnp.linspace(1e-4, nbands - 1, nbands, dtype=F32)
    pad = jnp.zeros((rows - 1 - 2 * nbands,), F32)
    zero1, ones, zeros = jnp.zeros((1,), F32), jnp.ones((nbands,), F32), jnp.zeros((nbands,), F32)
    band_col = jnp.concatenate([zero1, bands, bands, pad])[:, None]
    sgn_col = jnp.concatenate([zero1, ones, -ones, pad])[:, None]
    ph_col = jnp.concatenate([zero1, ones * (0.5 * math.pi), zeros, pad])[:, None]
    w1t = jnp.concatenate([w1.astype(F32), jnp.zeros((rows - w1.shape[0], fh), F32)], axis=0).T
    min_decay = math.log(DECAY_TARGET) / SLOW_DECAY_PCT
    max_decay = math.log(DECAY_TARGET) / FAST_DECAY_PCT
    deltas = jnp.abs(jnp.linspace(min_decay, max_decay, D_HYENA, dtype=F32))
    dl = jnp.concatenate([deltas, deltas])[None, :]
    args = [t[:, None], t[None, :], omega, band_col, sgn_col, ph_col, w1t, b1.reshape(fh, 1),
            jnp.swapaxes(w_inner, 1, 2), b_inner.reshape(FILTER_INNER, fh, 1), freq.reshape(fh, 1), w_out, dl]

    def cs(a):
        nd = a.ndim
        return pl.BlockSpec(a.shape, lambda i: (0,) * nd)

    in_specs = [pl.BlockSpec((tl, 1), lambda i: (i, 0)), pl.BlockSpec((1, tl), lambda i: (0, i)),
                pl.BlockSpec((1, tl), lambda i: (0, i))]
    in_specs += [cs(a) for a in args[3:]]
    return pl.pallas_call(
        _filter_kernel,
        grid=(n // tl,),
        in_specs=in_specs,
        out_specs=pl.BlockSpec((2, tl, D_HYENA), lambda i: (0, i, 0)),
        out_shape=jax.ShapeDtypeStruct((2, n, D_HYENA), F32),
        compiler_params=_cparams("parallel"),
        name="hyena_filter",
    )(*args)


def _dft_tables(n):
    m2 = FFT_M2
    m1 = 2 * n // m2
    m = 2 * n
    k1 = np.arange(m1, dtype=np.float64)[:, None]
    n1 = np.arange(m1 // 2, dtype=np.float64)[None, :]
    ang1 = 2.0 * np.pi * k1 * n1 / m1
    f1 = np.concatenate([np.cos(ang1), -np.sin(ang1)], axis=0)
    finv = np.concatenate([np.cos(ang1).T, -np.sin(ang1).T], axis=1)
    a = np.arange(m2, dtype=np.float64)
    ang2 = 2.0 * np.pi * a[:, None] * a[None, :] / m2
    angt = 2.0 * np.pi * k1 * a[None, :] / m
    return dict(
        m1=m1,
        f1=jnp.asarray(f1, F32), finv=jnp.asarray(finv, F32),
        f2r=jnp.asarray(np.cos(ang2), F32), f2i=jnp.asarray(-np.sin(ang2), F32),
        twr=jnp.asarray(np.cos(angt)[:, None, :], F32), twi=jnp.asarray(-np.sin(angt)[:, None, :], F32))


def _dft_a_kernel(x_ref, f_ref, o_ref):
    o_ref[0] = _dot(f_ref[...].astype(BF16), x_ref[0].astype(BF16)).astype(BF16)


def _dft_stage_a(x, f1):
    g, r, w = x.shape
    tn = min(FFT_TN, w)
    rows = f1.shape[0]
    return pl.pallas_call(
        _dft_a_kernel,
        grid=(g, w // tn),
        in_specs=[pl.BlockSpec((1, r, tn), lambda b, j: (b, 0, j)),
                  pl.BlockSpec(f1.shape, lambda b, j: (0, 0))],
        out_specs=pl.BlockSpec((1, rows, tn), lambda b, j: (b, 0, j)),
        out_shape=jax.ShapeDtypeStruct((g, rows, w), BF16),
        compiler_params=_cparams("parallel", "parallel"),
        name="dft_stage_a",
    )(x, f1)


def _twiddled_dft(f2r, f2i, twr, twi):
    gr = f2r * twr - f2i * twi
    gi = f2r * twi + f2i * twr
    return jnp.concatenate([jnp.concatenate([gr, -gi], axis=1),
                            jnp.concatenate([gi, gr], axis=1)], axis=0)


def _filter_spectrum_kernel(a_ref, f2r_ref, f2i_ref, twr_ref, twi_ref, k_ref, *, kc):
    m2 = FFT_M2
    for j in range(kc):
        g = _twiddled_dft(f2r_ref[...], f2i_ref[...], twr_ref[j], twi_ref[j]).astype(BF16)
        xf = _dot(g, a_ref[0, :, j].reshape(2 * m2, -1))
        xb = _dot(g, a_ref[1, :, j].reshape(2 * m2, -1))
        k_ref[0, j] = (xf[:m2] + xb[:m2]).astype(BF16)
        k_ref[1, j] = (xf[m2:] - xb[m2:]).astype(BF16)


def _filter_spectrum(a5, tabs):
    _, _, m1, m2, c = a5.shape
    kc = FFT_KC
    in_specs = [pl.BlockSpec((2, 2, kc, m2, c), lambda i: (0, 0, i, 0, 0)),
                pl.BlockSpec((m2, m2), lambda i: (0, 0)), pl.BlockSpec((m2, m2), lambda i: (0, 0)),
                pl.BlockSpec((kc, 1, m2), lambda i: (i, 0, 0)), pl.BlockSpec((kc, 1, m2), lambda i: (i, 0, 0))]
    return pl.pallas_call(
        functools.partial(_filter_spectrum_kernel, kc=kc),
        grid=(m1 // kc,),
        in_specs=in_specs,
        out_specs=pl.BlockSpec((2, kc, m2, c), lambda i: (0, i, 0, 0)),
        out_shape=jax.ShapeDtypeStruct((2, m1, m2, c), BF16),
        compiler_params=_cparams("parallel"),
        name="filter_spectrum",
    )(a5, tabs["f2r"], tabs["f2i"], tabs["twr"], tabs["twi"])


def _spectral_kernel(a_ref, kf_ref, f2r_ref, f2i_ref, twr_ref, twi_ref, b_ref, *, kc, nb):
    m2 = FFT_M2

    def forward(j):
        g32 = _twiddled_dft(f2r_ref[...], f2i_ref[...], twr_ref[j], twi_ref[j])
        g = g32.astype(BF16)
        xs = [_dot(g, a_ref[b, :, j].reshape(2 * m2, -1)) for b in range(nb)]
        return g32.T.astype(BF16), xs

    def inverse(j, gt, xs):
        kr = kf_ref[0, j].astype(F32)
        ki = kf_ref[1, j].astype(F32)
        for b in range(nb):
            xr, xi = xs[b][:m2], xs[b][m2:]
            y = jnp.concatenate([xr * kr - xi * ki, xr * ki + xi * kr], axis=0).astype(BF16)
            z = _dot(gt, y)
            b_ref[b, :, j] = z.reshape(2, m2, -1).astype(BF16)

    cur = forward(0)
    for j in range(1, kc):
        nxt = forward(j)
        inverse(j - 1, *cur)
        cur = nxt
    inverse(kc - 1, *cur)


def _spectral_multiply(a5, kf, tabs):
    nb, _, m1, m2, c = a5.shape
    kc = FFT_KC
    in_specs = [pl.BlockSpec((nb, 2, kc, m2, c), lambda i: (0, 0, i, 0, 0)),
                pl.BlockSpec((2, kc, m2, c), lambda i: (0, i, 0, 0)),
                pl.BlockSpec((m2, m2), lambda i: (0, 0)), pl.BlockSpec((m2, m2), lambda i: (0, 0)),
                pl.BlockSpec((kc, 1, m2), lambda i: (i, 0, 0)), pl.BlockSpec((kc, 1, m2), lambda i: (i, 0, 0))]
    return pl.pallas_call(
        functools.partial(_spectral_kernel, kc=kc, nb=nb),
        grid=(m1 // kc,),
        in_specs=in_specs,
        out_specs=pl.BlockSpec((nb, 2, kc, m2, c), lambda i: (0, 0, i, 0, 0)),
        out_shape=jax.ShapeDtypeStruct((nb, 2, m1, m2, c), BF16),
        compiler_params=_cparams("parallel"),
        name="spectral_multiply",
    )(a5, kf, tabs["f2r"], tabs["f2i"], tabs["twr"], tabs["twi"])


def _idft_a_kernel(b_ref, f_ref, x0_ref, u_ref, bias_ref, y_ref, *, inv_m):
    conv = _dot(f_ref[...].astype(BF16), b_ref[0]) * inv_m
    u = u_ref[0].astype(F32)
    y_ref[0] = (x0_ref[0].astype(F32) * (conv + u * bias_ref[...])).astype(BF16)


def _idft_stage_a(bm, finv, x0v, uv, bias_row, inv_m):
    g, rows, w = bm.shape
    r = finv.shape[0]
    tn = min(FFT_TN, w)
    return pl.pallas_call(
        functools.partial(_idft_a_kernel, inv_m=inv_m),
        grid=(g, w // tn),
        in_specs=[pl.BlockSpec((1, rows, tn), lambda b, j: (b, 0, j)),
                  pl.BlockSpec(finv.shape, lambda b, j: (0, 0)),
                  pl.BlockSpec((1, r, tn), lambda b, j: (b, 0, j)),
                  pl.BlockSpec((1, r, tn), lambda b, j: (b, 0, j)),
                  pl.BlockSpec((1, tn), lambda b, j: (0, 0))],
        out_specs=pl.BlockSpec((1, r, tn), lambda b, j: (b, 0, j)),
        out_shape=jax.ShapeDtypeStruct((g, r, w), BF16),
        compiler_params=_cparams("parallel", "parallel"),
        name="idft_stage_a",
    )(bm, finv, x0v, uv, bias_row)


def _hyena_mixer(u, x0, w1, b1, w_inner, b_inner, freq, w_out, bias):
    nb, n, c = u.shape
    tabs = _dft_tables(n)
    m1, m2 = tabs["m1"], FFT_M2
    w = m2 * c
    h = _hyena_filter(n, w1, b1, w_inner, b_inner, freq, w_out)
    ah = _dft_stage_a(h.reshape(2, m1 // 2, w), tabs["f1"])
    kf = _filter_spectrum(ah.reshape(2, 2, m1, m2, c), tabs)
    au = _dft_stage_a(u.reshape(nb, m1 // 2, w), tabs["f1"])
    bm = _spectral_multiply(au.reshape(nb, 2, m1, m2, c), kf, tabs)
    tn = min(FFT_TN, w)
    bias_row = jnp.tile(bias.astype(F32), tn // c)[None, :]
    y = _idft_stage_a(bm.reshape(nb, 2 * m1, w), tabs["finv"], x0.reshape(nb, m1 // 2, w),
                      u.reshape(nb, m1 // 2, w), bias_row, 1.0 / (2 * n))
    return y.reshape(nb, n, c)


def _rope_tables(n):
    rows = n // GRID_W
    row = jnp.broadcast_to(jnp.arange(rows, dtype=F32)[:, None], (rows, GRID_W)).reshape(-1)
    col = jnp.broadcast_to(jnp.arange(GRID_W, dtype=F32)[None, :], (rows, GRID_W)).reshape(-1)
    axis_dim = DIFF_HEAD_DIM // 2
    inv_freq = ROPE_BASE ** (-jnp.arange(0, axis_dim, 2, dtype=F32) / axis_dim)
    ang_r = row[:, None] * inv_freq
    ang_c = col[:, None] * inv_freq
    cos64 = jnp.concatenate([jnp.cos(ang_r), jnp.cos(ang_r), jnp.cos(ang_c), jnp.cos(ang_c)], axis=1)
    sin64 = jnp.concatenate([-jnp.sin(ang_r), jnp.sin(ang_r), -jnp.sin(ang_c), jnp.sin(ang_c)], axis=1)
    reps = LANES // DIFF_HEAD_DIM
    return (jnp.tile(cos64, (1, reps)), jnp.tile(sin64, (1, reps)), cos64.T, sin64.T)


def kernel(x, c, ctx, c_ctx, ada_w, ada_b, norm_g, ffn_w_gate, ffn_w_up, ffn_w_down, w_in, w_out,
           hyena_conv_w, hyena_conv_b, filt_w1, filt_b1, filt_w_inner, filt_b_inner, filt_sin_freq,
           filt_w_out, hyena_bias, diff_lambda, diff_subln_g, final_g):
    nb, n, d = x.shape
    layer = 0
    hy_end = HYENA_N_PROJ * D_HYENA
    q_end = hy_end + D_DIFF
    k_end = q_end + D_DIFF

    cc = jnp.concatenate([c, c_ctx[None, :], jnp.zeros((8 - nb - 1, d), F32)], axis=0)
    mod = _modulation(cc, ada_w[layer], ada_b[layer])
    mod_lat = [mod[:nb, j * d:(j + 1) * d][:, None, :] for j in range(N_ADA)]
    mod_ctx = [mod[nb:nb + 1, j * d:(j + 1) * d][:, None, :] for j in range(N_ADA)]

    def ffn_weights(i):
        return (ffn_w_gate[layer, i].astype(BF16), ffn_w_up[layer, i].astype(BF16),
                ffn_w_down[layer, i].astype(BF16))

    ffn0 = ffn_weights(0)
    h_lat = _ffn_half_step(x, norm_g[layer, 0], mod_lat[0], mod_lat[1], mod_lat[2], *ffn0)
    h_ctx = _ffn_half_step(ctx, norm_g[layer, 0], mod_ctx[0], mod_ctx[1], mod_ctx[2], *ffn0)

    w = w_in[layer]
    wzk = jnp.concatenate([w[:, :hy_end], w[:, q_end:k_end]], axis=1).astype(BF16)
    wqv = jnp.concatenate([w[:, hy_end:q_end], w[:, k_end:]], axis=1).T.astype(BF16)
    ck, sk, cq, sq = _rope_tables(n)
    u_hy, x0_hy, k_lat, qT, vT4 = _input_projection(
        h_lat, norm_g[layer, 1], mod_lat[3], mod_lat[4], wzk, wqv, ck, sk, cq, sq,
        hyena_conv_w[layer], hyena_conv_b[layer])
    k_ctx, vT_ctx = _context_projection(h_ctx, norm_g[layer, 1], mod_ctx[3], mod_ctx[4],
                                        w[:, q_end:k_end].astype(BF16), w[:, k_end:].T.astype(BF16))

    y_diff = _diff_attention(qT, k_lat, vT4, k_ctx, vT_ctx, diff_lambda[layer].astype(F32),
                             diff_subln_g[layer])
    y_hy = _hyena_mixer(u_hy, x0_hy, filt_w1[layer], filt_b1[layer], filt_w_inner[layer],
                        filt_b_inner[layer], filt_sin_freq[layer], filt_w_out[layer], hyena_bias[layer])

    wo = w_out[layer].astype(BF16)
    mixer = (y_hy, y_diff, wo[:D_HYENA], wo[D_HYENA:], mod_lat[5])
    return _ffn_half_step(h_lat, norm_g[layer, 2], mod_lat[6], mod_lat[7], mod_lat[8],
                          *ffn_weights(1), mixer=mixer, final_g=final_g)
```

```python
import functools
import math

import numpy as np
import jax
import jax.numpy as jnp
from jax import lax
from jax.experimental import pallas as pl
from jax.experimental.pallas import tpu as pltpu

F32 = jnp.float32
BF16 = jnp.bfloat16

N_ADA = 9
FFN_HIDDEN = 2816
D_HYENA = 512
HYENA_N_PROJ = 3
FILTER_EMB_BANDS = 16
FILTER_HIDDEN = 64
FILTER_EMB_ROWS = 40
FILTER_INNER = 2
DECAY_TARGET = 1e-2
FAST_DECAY_PCT = 0.3
SLOW_DECAY_PCT = 1.5
N_DIFF_HEADS = 4
DIFF_HEAD_DIM = 64
D_DIFF = N_DIFF_HEADS * 2 * DIFF_HEAD_DIM
GRID_W = 64
ROPE_BASE = 10000.0
RMS_EPS = 1e-6
SUBLN_EPS = 1e-5
LAM_INIT = 0.8 - 0.6 * math.exp(-0.3 * 0)

LANES = 128
BF16_ROWS = 16
V7X_VMEM_BYTES = 64 * 2**20
VMEM_LIMIT = 56 * 2**20

FFN_TM = 512
FFN_TF = 256
PROJ_TM = 512
ATT_TK = 512
ATT_STEPS = 2
ATT_TQ = 2048
FFT_M2 = 128
FFT_TN = 4096
FFT_KC = 8
FILT_TL = 1024
NEG_BIG = -1e30

LOG2E = 1.4426950408889634


def _cparams(*sem):
    return pltpu.CompilerParams(dimension_semantics=sem, vmem_limit_bytes=VMEM_LIMIT)


def _dot(a, b):
    return jnp.dot(a, b, preferred_element_type=F32)


def _dot3(a, b):
    a_hi = a.astype(BF16)
    a_lo = (a - a_hi.astype(F32)).astype(BF16)
    b_hi = b.astype(BF16)
    b_lo = (b - b_hi.astype(F32)).astype(BF16)
    return _dot(a_hi, b_hi) + (_dot(a_lo, b_hi) + _dot(a_hi, b_lo))


def _silu(x):
    return x * (1.0 / (1.0 + jnp.exp(-x)))


def _adaln(s, g, shift, scale):
    ms = jnp.mean(s * s, axis=-1, keepdims=True)
    return (s * lax.rsqrt(ms + RMS_EPS) * g) * (1.0 + scale) + shift


def _mod_kernel(c_ref, w_ref, b_ref, o_ref):
    o_ref[...] = _dot3(_silu(c_ref[...]), w_ref[...]) + b_ref[...]


def _modulation(cc, ada_w, ada_b):
    rows, d = cc.shape
    n = ada_w.shape[1]
    tn = 768
    return pl.pallas_call(
        _mod_kernel,
        grid=(n // tn,),
        in_specs=[pl.BlockSpec((rows, d), lambda j: (0, 0)),
                  pl.BlockSpec((d, tn), lambda j: (0, j)),
                  pl.BlockSpec((1, tn), lambda j: (0, j))],
        out_specs=pl.BlockSpec((rows, tn), lambda j: (0, j)),
        out_shape=jax.ShapeDtypeStruct((rows, n), F32),
        compiler_params=_cparams("parallel"),
        name="modulation",
    )(cc, ada_w, ada_b.reshape(1, n))


def _mod_spec(arr, nb):
    d = arr.shape[-1]
    if arr.shape[0] == nb:
        return pl.BlockSpec((1, 1, d), lambda b, i: (b, 0, 0))
    return pl.BlockSpec((1, 1, d), lambda b, i: (0, 0, 0))


def _const_spec(arr):
    nd = arr.ndim
    return pl.BlockSpec(arr.shape, lambda b, i: (0,) * nd)


def _ffn_kernel(*refs, n_chunks, tf, mix, final):
    refs = list(refs)
    o_ref = refs.pop()
    s_ref, g_ref, sh_ref, sc_ref, gt_ref, wg_ref, wu_ref, wd_ref = refs[:8]
    rest = refs[8:]
    s = s_ref[0]
    if mix:
        yh_ref, yd_ref, woa_ref, wob_ref, gm_ref = rest[:5]
        rest = rest[5:]
        s = s + gm_ref[0] * (_dot(yh_ref[0], woa_ref[...]) + _dot(yd_ref[0], wob_ref[...]))
    ub = _adaln(s, g_ref[...], sh_ref[0], sc_ref[0]).astype(BF16)
    acc = jnp.zeros(s.shape, F32)
    for f in range(n_chunks):
        cols = slice(f * tf, (f + 1) * tf)
        gate = _dot(ub, wg_ref[:, cols])
        up = _dot(ub, wu_ref[:, cols])
        a = (_silu(gate) * up).astype(BF16)
        acc = acc + _dot(a, wd_ref[cols, :])
    out = s + (0.5 * gt_ref[0]) * acc
    if final:
        ms = jnp.mean(out * out, axis=-1, keepdims=True)
        out = out * lax.rsqrt(ms + RMS_EPS) * rest[0][...]
    o_ref[0] = out


def _ffn_half_step(s, g, shift, scale, gate, wg, wu, wd, mixer=None, final_g=None):
    nb, t, d = s.shape
    tm = min(FFN_TM, t)
    args = [s, g.reshape(1, d), shift, scale, gate, wg, wu, wd]
    in_specs = [pl.BlockSpec((1, tm, d), lambda b, i: (b, i, 0)),
                _const_spec(args[1]),
                _mod_spec(shift, nb), _mod_spec(scale, nb), _mod_spec(gate, nb),
                _const_spec(wg), _const_spec(wu), _const_spec(wd)]
    if mixer is not None:
        yh, yd, woa, wob, gm = mixer
        args += [yh, yd, woa, wob, gm]
        in_specs += [pl.BlockSpec((1, tm, yh.shape[2]), lambda b, i: (b, i, 0)),
                     pl.BlockSpec((1, tm, yd.shape[2]), lambda b, i: (b, i, 0)),
                     _const_spec(woa), _const_spec(wob), _mod_spec(gm, nb)]
    if final_g is not None:
        args.append(final_g.reshape(1, d))
        in_specs.append(_const_spec(args[-1]))
    kern = functools.partial(_ffn_kernel, n_chunks=wg.shape[1] // FFN_TF, tf=FFN_TF,
                             mix=mixer is not None, final=final_g is not None)
    return pl.pallas_call(
        kern,
        grid=(nb, t // tm),
        in_specs=in_specs,
        out_specs=pl.BlockSpec((1, tm, d), lambda b, i: (b, i, 0)),
        out_shape=jax.ShapeDtypeStruct((nb, t, d), F32),
        compiler_params=_cparams("parallel", "parallel"),
        name="ffn_mix_final" if mixer is not None else "ffn",
    )(*args)


def _rope_partner(x, axis):
    n = x.shape[axis]
    idx = lax.broadcasted_iota(jnp.int32, x.shape, axis)
    first_half = (idx & 16) == 0
    return jnp.where(first_half, pltpu.roll(x, n - 16, axis), pltpu.roll(x, 16, axis))


def _inproj_kernel(s_ref, sp_ref, sn_ref, g_ref, sh_ref, sc_ref, wzk_ref, wqv_ref, ck_ref, sk_ref,
                   cq_ref, sq_ref, cw_ref, cb_ref, u_ref, x0_ref, k_ref, qT_ref, vT_ref, *, q_scale):
    i = pl.program_id(1)
    last = pl.num_programs(1) - 1
    tm = s_ref.shape[1]
    rows = jnp.concatenate([s_ref[0], sp_ref[0], sn_ref[0]], axis=0)
    ub_ext = _adaln(rows, g_ref[...], sh_ref[0], sc_ref[0]).astype(BF16)
    ub = ub_ext[:tm]
    zk = _dot(ub_ext, wzk_ref[...])
    hy = HYENA_N_PROJ * D_HYENA
    z = zk[:tm, :hy]
    prev = jnp.where(i == 0, 0.0, zk[tm + 7:tm + 8, :hy])
    nxt = jnp.where(i == last, 0.0, zk[tm + 8:tm + 9, :hy])
    row = lax.broadcasted_iota(jnp.int32, z.shape, 0)
    z_dn = jnp.where(row == 0, prev, pltpu.roll(z, 1, 0))
    z_up = jnp.where(row == tm - 1, nxt, pltpu.roll(z, tm - 1, 0))
    y = z_dn * cw_ref[0:1] + z * cw_ref[1:2] + z_up * cw_ref[2:3] + cb_ref[...]
    c = D_HYENA
    x0_ref[0] = y[:, :c].astype(BF16)
    u_ref[0] = (y[:, c:2 * c] * y[:, 2 * c:]).astype(BF16)
    k = zk[:tm, hy:]
    reps = D_DIFF // LANES
    ck = jnp.concatenate([ck_ref[...]] * reps, axis=1)
    sk = jnp.concatenate([sk_ref[...]] * reps, axis=1)
    k_ref[0] = (k * ck + _rope_partner(k, 1) * sk).astype(BF16)
    qv = lax.dot_general(wqv_ref[...], ub, (((1,), (1,)), ((), ())), preferred_element_type=F32)
    q = qv[:D_DIFF]
    groups = D_DIFF // DIFF_HEAD_DIM
    cq = jnp.broadcast_to(cq_ref[...][None], (groups, DIFF_HEAD_DIM, tm)).reshape(D_DIFF, tm)
    sq = jnp.broadcast_to(sq_ref[...][None], (groups, DIFF_HEAD_DIM, tm)).reshape(D_DIFF, tm)
    q = (q * cq + _rope_partner(q, 0) * sq) * q_scale
    qT_ref[0] = q.astype(BF16)
    v = qv[D_DIFF:].astype(BF16)
    tk = vT_ref.shape[3]
    for j in range(vT_ref.shape[1]):
        vT_ref[0, j] = v[:, j * tk:(j + 1) * tk]


def _input_projection(h, g, shift, scale, wzk, wqv, ck, sk, cq, sq, conv_w, conv_b):
    nb, n, d = h.shape
    tm = PROJ_TM
    r, nblk8 = tm // 8, n // 8
    c3 = conv_w.shape[1]
    q_scale = DIFF_HEAD_DIM ** -0.5 * LOG2E
    args = [h, h, h, g.reshape(1, d), shift, scale, wzk, wqv, ck, sk, cq, sq, conv_w, conv_b.reshape(1, c3)]
    in_specs = [pl.BlockSpec((1, tm, d), lambda b, i: (b, i, 0)),
                pl.BlockSpec((1, 8, d), lambda b, i: (b, jnp.maximum(i * r - 1, 0), 0)),
                pl.BlockSpec((1, 8, d), lambda b, i: (b, jnp.minimum((i + 1) * r, nblk8 - 1), 0)),
                _const_spec(args[3]), _mod_spec(shift, nb), _mod_spec(scale, nb),
                _const_spec(wzk), _const_spec(wqv),
                pl.BlockSpec((tm, LANES), lambda b, i: (i, 0)),
                pl.BlockSpec((tm, LANES), lambda b, i: (i, 0)),
                pl.BlockSpec((DIFF_HEAD_DIM, tm), lambda b, i: (0, i)),
                pl.BlockSpec((DIFF_HEAD_DIM, tm), lambda b, i: (0, i)),
                _const_spec(conv_w), _const_spec(args[-1])]
    out_shape = (jax.ShapeDtypeStruct((nb, n, D_HYENA), BF16),
                 jax.ShapeDtypeStruct((nb, n, D_HYENA), BF16),
                 jax.ShapeDtypeStruct((nb, n, D_DIFF), BF16),
                 jax.ShapeDtypeStruct((nb, D_DIFF, n), BF16),
                 jax.ShapeDtypeStruct((nb, n // ATT_TK, D_DIFF, ATT_TK), BF16))
    out_specs = (pl.BlockSpec((1, tm, D_HYENA), lambda b, i: (b, i, 0)),
                 pl.BlockSpec((1, tm, D_HYENA), lambda b, i: (b, i, 0)),
                 pl.BlockSpec((1, tm, D_DIFF), lambda b, i: (b, i, 0)),
                 pl.BlockSpec((1, D_DIFF, tm), lambda b, i: (b, 0, i)),
                 pl.BlockSpec((1, tm // ATT_TK, D_DIFF, ATT_TK), lambda b, i: (b, i, 0, 0)))
    return pl.pallas_call(
        functools.partial(_inproj_kernel, q_scale=q_scale),
        grid=(nb, n // tm),
        in_specs=in_specs, out_specs=out_specs, out_shape=out_shape,
        compiler_params=_cparams("parallel", "parallel"),
        name="input_projection",
    )(*args)


def _inproj_ctx_kernel(s_ref, g_ref, sh_ref, sc_ref, wk_ref, wv_ref, k_ref, vT_ref):
    ub = _adaln(s_ref[0], g_ref[...], sh_ref[0], sc_ref[0]).astype(BF16)
    k_ref[0] = _dot(ub, wk_ref[...]).astype(BF16)
    vT_ref[0] = lax.dot_general(wv_ref[...], ub, (((1,), (1,)), ((), ())),
                                preferred_element_type=F32).astype(BF16)


def _context_projection(h, g, shift, scale, wk, wvT):
    nb, c, d = h.shape
    args = [h, g.reshape(1, d), shift, scale, wk, wvT]
    in_specs = [pl.BlockSpec((1, c, d), lambda b, i: (b, 0, 0)),
                _const_spec(args[1]), _mod_spec(shift, nb), _mod_spec(scale, nb),
                _const_spec(wk), _const_spec(wvT)]
    return pl.pallas_call(
        _inproj_ctx_kernel,
        grid=(nb, 1),
        in_specs=in_specs,
        out_specs=(pl.BlockSpec((1, c, D_DIFF), lambda b, i: (b, 0, 0)),
                   pl.BlockSpec((1, D_DIFF, c), lambda b, i: (b, 0, 0))),
        out_shape=(jax.ShapeDtypeStruct((nb, c, D_DIFF), BF16),
                   jax.ShapeDtypeStruct((nb, D_DIFF, c), BF16)),
        compiler_params=_cparams("parallel", "parallel"),
        name="context_projection",
    )(*args)


def _attn_kernel(qT_ref, k_ref, vT_ref, kc_ref, vTc_ref, lp_ref, sg_ref, o_ref,
                 acc0_ref, acc1_ref, qa_ref, qb_ref, *sp_refs, n_chunks, tk):
    d = DIFF_HEAD_DIM
    s_refs = (sp_refs[0:2], sp_refs[2:4])
    p_refs = (sp_refs[4:6], sp_refs[6:8])
    q = qT_ref[0]
    tq = q.shape[1]
    row = lax.broadcasted_iota(jnp.int32, q.shape, 0)
    zero = jnp.zeros_like(q)
    qa_ref[...] = jnp.where(row < d, q, zero)
    qb_ref[...] = jnp.where(row < d, zero, q)
    acc0_ref[...] = jnp.zeros_like(acc0_ref)
    acc1_ref[...] = jnp.zeros_like(acc1_ref)
    q_refs = (qa_ref, qb_ref)
    acc_refs = (acc0_ref, acc1_ref)

    def with_ones(vc):
        r = lax.broadcasted_iota(jnp.int32, (BF16_ROWS, vc.shape[1]), 0)
        return jnp.concatenate([vc, jnp.where(r == 0, 1.0, 0.0).astype(BF16)], axis=0)

    def scores(kc, slot):
        cmax = []
        for h in range(2):
            s = _dot(kc, q_refs[h][...])
            s_refs[slot][h][...] = s
            cmax.append(jnp.max(s, axis=0, keepdims=True))
        return tuple(cmax)

    def probs(s, cmax, m):
        m_new = jnp.maximum(m, cmax)
        return jnp.exp2((s - m_new).astype(BF16)), m_new, jnp.exp2(m - m_new)

    def softmax_stage(slot, cmax, stats):
        out = []
        for h in range(2):
            p, m, alpha = probs(s_refs[slot][h][...], cmax[h], stats[h][0])
            p_refs[slot][h][...] = p
            out.append((m, alpha))
        return tuple(out)

    def values(vc, slot, stats):
        for h in range(2):
            acc_refs[h][...] = stats[h][1] * acc_refs[h][...] + _dot(vc, p_refs[slot][h][...])

    def lat_keys(t):
        return k_ref[0, pl.ds(pl.multiple_of(t * tk, tk), tk), :]

    def step(t, slot, cmax_other, stats, with_scores=True):
        cmax = scores(lat_keys(t + 2), slot) if with_scores else None
        values(with_ones(vT_ref[0, t]), slot, stats)
        new_stats = softmax_stage(1 - slot, cmax_other, stats)
        return cmax, new_stats

    neg = jnp.full((1, tq), NEG_BIG, F32)
    one = jnp.ones((1, tq), F32)
    cmax0 = scores(lat_keys(0), 0)
    cmax1 = scores(lat_keys(1), 1)
    carry = (cmax1, softmax_stage(0, cmax0, ((neg, one), (neg, one))))

    def body(i, carry):
        for j in range(ATT_STEPS):
            carry = step(ATT_STEPS * i + j, j % 2, *carry)
        return carry

    n_loop = (n_chunks - 2) // ATT_STEPS
    carry = lax.fori_loop(0, n_loop, body, carry)
    for t in range(n_loop * ATT_STEPS, n_chunks - 1):
        carry = step(t, t % 2, *carry, with_scores=t + 2 < n_chunks)
    stats = carry[1]
    values(with_ones(vT_ref[0, n_chunks - 1]), (n_chunks - 1) % 2, stats)

    kc = kc_ref[0]
    vc = with_ones(vTc_ref[0])
    for h in range(2):
        s = _dot(kc, q_refs[h][...])
        p, _, alpha = probs(s, jnp.max(s, axis=0, keepdims=True), stats[h][0])
        acc_refs[h][...] = alpha * acc_refs[h][...] + _dot(vc, p)

    lp = lp_ref[...]
    lam = (jnp.exp(jnp.sum(lp[0:1] * lp[1:2], axis=1, keepdims=True))
           - jnp.exp(jnp.sum(lp[2:3] * lp[3:4], axis=1, keepdims=True)) + LAM_INIT)
    hd = 2 * d
    o0 = acc0_ref[:hd, :] / acc0_ref[hd:hd + 1, :]
    o1 = acc1_ref[:hd, :] / acc1_ref[hd:hd + 1, :]
    oT = o0 - lam * o1
    ms = jnp.mean(oT * oT, axis=0, keepdims=True)
    oT = oT * lax.rsqrt(ms + SUBLN_EPS)
    o_ref[0] = ((oT.T * sg_ref[...]) * (1.0 - LAM_INIT)).astype(BF16)


def _diff_attention(qT, k, vT4, k_ctx, vT_ctx, lam_params, subln_g):
    nb, n, _ = k.shape
    hd = 2 * DIFF_HEAD_DIM
    n_chunks, tk = vT4.shape[1], vT4.shape[3]
    c = k_ctx.shape[1]
    tq = ATT_TQ
    in_specs = [pl.BlockSpec((1, hd, tq), lambda b, h, i: (b, h, i)),
                pl.BlockSpec((1, n, hd), lambda b, h, i: (b, 0, h)),
                pl.BlockSpec((1, n_chunks, hd, tk), lambda b, h, i: (b, 0, h, 0)),
                pl.BlockSpec((1, c, hd), lambda b, h, i: (b, 0, h)),
                pl.BlockSpec((1, hd, c), lambda b, h, i: (b, h, 0)),
                pl.BlockSpec(lam_params.shape, lambda b, h, i: (0, 0)),
                pl.BlockSpec((1, hd), lambda b, h, i: (0, 0))]
    return pl.pallas_call(
        functools.partial(_attn_kernel, n_chunks=n_chunks, tk=tk),
        grid=(nb, N_DIFF_HEADS, n // tq),
        in_specs=in_specs,
        out_specs=pl.BlockSpec((1, tq, hd), lambda b, h, i: (b, i, h)),
        out_shape=jax.ShapeDtypeStruct((nb, n, D_DIFF), BF16),
        scratch_shapes=([pltpu.VMEM((hd + BF16_ROWS, tq), F32)] * 2 + [pltpu.VMEM((hd, tq), BF16)] * 2
                        + [pltpu.VMEM((tk, tq), F32)] * 4 + [pltpu.VMEM((tk, tq), BF16)] * 4),
        compiler_params=_cparams("parallel", "parallel", "arbitrary"),
        name="diff_attention",
    )(qT, k, vT4, k_ctx, vT_ctx, lam_params, subln_g.reshape(1, hd))


def _filter_kernel(tc_ref, tr_ref, om_ref, band_ref, sgn_ref, ph_ref, w1_ref, b1_ref, wi_ref, bi_ref,
                   fr_ref, wo_ref, dl_ref, h_ref):
    tr = tr_ref[...]
    rows, tl = FILTER_EMB_ROWS, tr.shape[1]
    r = lax.broadcasted_iota(jnp.int32, (rows, tl), 0)
    trig = jnp.sin(sgn_ref[...] * (band_ref[...] * om_ref[...]) + ph_ref[...])
    emb = jnp.where(r == 0, tr, jnp.where(r <= 2 * FILTER_EMB_BANDS, trig, 0.0))
    fr = fr_ref[...]
    h = jnp.sin(fr * (_dot3(w1_ref[...], emb) + b1_ref[...]))
    for i in range(FILTER_INNER):
        h = jnp.sin(fr * (_dot3(wi_ref[i], h) + bi_ref[i]))
    h = _dot3(h.T, wo_ref[...]) * jnp.exp(-tc_ref[...] * dl_ref[...])
    c = D_HYENA
    h_ref[0] = h[:, :c]
    first = (pl.program_id(0) == 0) & (lax.broadcasted_iota(jnp.int32, (tl, c), 0) == 0)
    h_ref[1] = jnp.where(first, 0.0, h[:, c:])


def _hyena_filter(n, w1, b1, w_inner, b_inner, freq, w_out):
    tl = min(FILT_TL, n)
    t = jnp.linspace(0.0, 1.0, n, dtype=F32)
    omega = ((2.0 * math.pi / n) * jnp.arange(n, dtype=F32))[None, :]
    nbands, rows, fh = FILTER_EMB_BANDS, FILTER_EMB_ROWS, FILTER_HIDDEN
    bands = jnp.linspace(1e-4, nbands - 1, nbands, dtype=F32)
    pad = jnp.zeros((rows - 1 - 2 * nbands,), F32)
    zero1, ones, zeros = jnp.zeros((1,), F32), jnp.ones((nbands,), F32), jnp.zeros((nbands,), F32)
    band_col = jnp.concatenate([zero1, bands, bands, pad])[:, None]
    sgn_col = jnp.concatenate([zero1, ones, -ones, pad])[:, None]
    ph_col = jnp.concatenate([zero1, ones * (0.5 * math.pi), zeros, pad])[:, None]
    w1t = jnp.concatenate([w1.astype(F32), jnp.zeros((rows - w1.shape[0], fh), F32)], axis=0).T
    min_decay = math.log(DECAY_TARGET) / SLOW_DECAY_PCT
    max_decay = math.log(DECAY_TARGET) / FAST_DECAY_PCT
    deltas = jnp.abs(jnp.linspace(min_decay, max_decay, D_HYENA, dtype=F32))
    dl = jnp.concatenate([deltas, deltas])[None, :]
    args = [t[:, None], t[None, :], omega, band_col, sgn_col, ph_col, w1t, b1.reshape(fh, 1),
            jnp.swapaxes(w_inner, 1, 2), b_inner.reshape(FILTER_INNER, fh, 1), freq.reshape(fh, 1), w_out, dl]

    def cs(a):
        nd = a.ndim
        return pl.BlockSpec(a.shape, lambda i: (0,) * nd)

    in_specs = [pl.BlockSpec((tl, 1), lambda i: (i, 0)), pl.BlockSpec((1, tl), lambda i: (0, i)),
                pl.BlockSpec((1, tl), lambda i: (0, i))]
    in_specs += [cs(a) for a in args[3:]]
    return pl.pallas_call(
        _filter_kernel,
        grid=(n // tl,),
        in_specs=in_specs,
        out_specs=pl.BlockSpec((2, tl, D_HYENA), lambda i: (0, i, 0)),
        out_shape=jax.ShapeDtypeStruct((2, n, D_HYENA), F32),
        compiler_params=_cparams("parallel"),
        name="hyena_filter",
    )(*args)


def _dft_tables(n):
    m2 = FFT_M2
    m1 = 2 * n // m2
    m = 2 * n
    mk = m1 // 2 + FFT_KC
    k1 = np.arange(mk, dtype=np.float64)[:, None]
    n1 = np.arange(m1 // 2, dtype=np.float64)[None, :]
    ang1 = 2.0 * np.pi * k1 * n1 / m1
    f1 = np.concatenate([np.cos(ang1), -np.sin(ang1)], axis=0)
    wgt = np.where((k1 == 0) | (k1 == m1 // 2), 1.0, np.where(k1 < m1 // 2, 2.0, 0.0))
    finv = np.concatenate([(wgt * np.cos(ang1)).T, -(wgt * np.sin(ang1)).T], axis=1)
    a = np.arange(m2, dtype=np.float64)
    ang2 = 2.0 * np.pi * a[:, None] * a[None, :] / m2
    angt = 2.0 * np.pi * k1 * a[None, :] / m
    return dict(
        m1=m1, mk=mk,
        f1=jnp.asarray(f1, F32), finv=jnp.asarray(finv, F32),
        f2r=jnp.asarray(np.cos(ang2), F32), f2i=jnp.asarray(-np.sin(ang2), F32),
        twr=jnp.asarray(np.cos(angt)[:, None, :], F32), twi=jnp.asarray(-np.sin(angt)[:, None, :], F32))


def _dft_a_kernel(x_ref, f_ref, o_ref):
    o_ref[0] = _dot(f_ref[...].astype(BF16), x_ref[0].astype(BF16)).astype(BF16)


def _dft_stage_a(x, f1):
    g, r, w = x.shape
    tn = min(FFT_TN, w)
    rows = f1.shape[0]
    return pl.pallas_call(
        _dft_a_kernel,
        grid=(g, w // tn),
        in_specs=[pl.BlockSpec((1, r, tn), lambda b, j: (b, 0, j)),
                  pl.BlockSpec(f1.shape, lambda b, j: (0, 0))],
        out_specs=pl.BlockSpec((1, rows, tn), lambda b, j: (b, 0, j)),
        out_shape=jax.ShapeDtypeStruct((g, rows, w), BF16),
        compiler_params=_cparams("parallel", "parallel"),
        name="dft_stage_a",
    )(x, f1)


def _twiddled_dft(f2r, f2i, twr, twi):
    gr = f2r * twr - f2i * twi
    gi = f2r * twi + f2i * twr
    return jnp.concatenate([jnp.concatenate([gr, -gi], axis=1),
                            jnp.concatenate([gi, gr], axis=1)], axis=0)


def _filter_spectrum_kernel(a_ref, f2r_ref, f2i_ref, twr_ref, twi_ref, k_ref, *, kc):
    m2 = FFT_M2
    for j in range(kc):
        g = _twiddled_dft(f2r_ref[...], f2i_ref[...], twr_ref[j], twi_ref[j]).astype(BF16)
        xf = _dot(g, a_ref[0, :, j].reshape(2 * m2, -1))
        xb = _dot(g, a_ref[1, :, j].reshape(2 * m2, -1))
        k_ref[0, j] = (xf[:m2] + xb[:m2]).astype(BF16)
        k_ref[1, j] = (xf[m2:] - xb[m2:]).astype(BF16)


def _filter_spectrum(a5, tabs):
    _, _, m1, m2, c = a5.shape
    kc = FFT_KC
    in_specs = [pl.BlockSpec((2, 2, kc, m2, c), lambda i: (0, 0, i, 0, 0)),
                pl.BlockSpec((m2, m2), lambda i: (0, 0)), pl.BlockSpec((m2, m2), lambda i: (0, 0)),
                pl.BlockSpec((kc, 1, m2), lambda i: (i, 0, 0)), pl.BlockSpec((kc, 1, m2), lambda i: (i, 0, 0))]
    return pl.pallas_call(
        functools.partial(_filter_spectrum_kernel, kc=kc),
        grid=(m1 // kc,),
        in_specs=in_specs,
        out_specs=pl.BlockSpec((2, kc, m2, c), lambda i: (0, i, 0, 0)),
        out_shape=jax.ShapeDtypeStruct((2, m1, m2, c), BF16),
        compiler_params=_cparams("parallel"),
        name="filter_spectrum",
    )(a5, tabs["f2r"], tabs["f2i"], tabs["twr"], tabs["twi"])


def _spectral_kernel(a_ref, kf_ref, f2r_ref, f2i_ref, twr_ref, twi_ref, b_ref, *, kc, nb):
    m2 = FFT_M2

    def forward(j):
        g32 = _twiddled_dft(f2r_ref[...], f2i_ref[...], twr_ref[j], twi_ref[j])
        g = g32.astype(BF16)
        xs = [_dot(g, a_ref[b, :, j].reshape(2 * m2, -1)) for b in range(nb)]
        return g32.T.astype(BF16), xs

    def inverse(j, gt, xs):
        kr = kf_ref[0, j].astype(F32)
        ki = kf_ref[1, j].astype(F32)
        for b in range(nb):
            xr, xi = xs[b][:m2], xs[b][m2:]
            y = jnp.concatenate([xr * kr - xi * ki, xr * ki + xi * kr], axis=0).astype(BF16)
            z = _dot(gt, y)
            b_ref[b, :, j] = z.reshape(2, m2, -1).astype(BF16)

    cur = forward(0)
    for j in range(1, kc):
        nxt = forward(j)
        inverse(j - 1, *cur)
        cur = nxt
    inverse(kc - 1, *cur)


def _spectral_multiply(a5, kf, tabs):
    nb, _, m1, m2, c = a5.shape
    kc = FFT_KC
    in_specs = [pl.BlockSpec((nb, 2, kc, m2, c), lambda i: (0, 0, i, 0, 0)),
                pl.BlockSpec((2, kc, m2, c), lambda i: (0, i, 0, 0)),
                pl.BlockSpec((m2, m2), lambda i: (0, 0)), pl.BlockSpec((m2, m2), lambda i: (0, 0)),
                pl.BlockSpec((kc, 1, m2), lambda i: (i, 0, 0)), pl.BlockSpec((kc, 1, m2), lambda i: (i, 0, 0))]
    return pl.pallas_call(
        functools.partial(_spectral_kernel, kc=kc, nb=nb),
        grid=(m1 // kc,),
        in_specs=in_specs,
        out_specs=pl.BlockSpec((nb, 2, kc, m2, c), lambda i: (0, 0, i, 0, 0)),
        out_shape=jax.ShapeDtypeStruct((nb, 2, m1, m2, c), BF16),
        compiler_params=_cparams("parallel"),
        name="spectral_multiply",
    )(a5, kf, tabs["f2r"], tabs["f2i"], tabs["twr"], tabs["twi"])


def _idft_a_kernel(b_ref, f_ref, x0_ref, u_ref, bias_ref, y_ref, *, inv_m):
    conv = _dot(f_ref[...].astype(BF16), b_ref[0]) * inv_m
    u = u_ref[0].astype(F32)
    y_ref[0] = (x0_ref[0].astype(F32) * (conv + u * bias_ref[...])).astype(BF16)


def _idft_stage_a(bm, finv, x0v, uv, bias_row, inv_m):
    g, rows, w = bm.shape
    r = finv.shape[0]
    tn = min(FFT_TN, w)
    return pl.pallas_call(
        functools.partial(_idft_a_kernel, inv_m=inv_m),
        grid=(g, w // tn),
        in_specs=[pl.BlockSpec((1, rows, tn), lambda b, j: (b, 0, j)),
                  pl.BlockSpec(finv.shape, lambda b, j: (0, 0)),
                  pl.BlockSpec((1, r, tn), lambda b, j: (b, 0, j)),
                  pl.BlockSpec((1, r, tn), lambda b, j: (b, 0, j)),
                  pl.BlockSpec((1, tn), lambda b, j: (0, 0))],
        out_specs=pl.BlockSpec((1, r, tn), lambda b, j: (b, 0, j)),
        out_shape=jax.ShapeDtypeStruct((g, r, w), BF16),
        compiler_params=_cparams("parallel", "parallel"),
        name="idft_stage_a",
    )(bm, finv, x0v, uv, bias_row)


def _hyena_mixer(u, x0, w1, b1, w_inner, b_inner, freq, w_out, bias):
    nb, n, c = u.shape
    tabs = _dft_tables(n)
    m1, mk, m2 = tabs["m1"], tabs["mk"], FFT_M2
    w = m2 * c
    h = _hyena_filter(n, w1, b1, w_inner, b_inner, freq, w_out)
    ah = _dft_stage_a(h.reshape(2, m1 // 2, w), tabs["f1"])
    kf = _filter_spectrum(ah.reshape(2, 2, mk, m2, c), tabs)
    au = _dft_stage_a(u.reshape(nb, m1 // 2, w), tabs["f1"])
    bm = _spectral_multiply(au.reshape(nb, 2, mk, m2, c), kf, tabs)
    tn = min(FFT_TN, w)
    bias_row = jnp.tile(bias.astype(F32), tn // c)[None, :]
    y = _idft_stage_a(bm.reshape(nb, 2 * mk, w), tabs["finv"], x0.reshape(nb, m1 // 2, w),
                      u.reshape(nb, m1 // 2, w), bias_row, 1.0 / (2 * n))
    return y.reshape(nb, n, c)


def _rope_tables(n):
    rows = n // GRID_W
    row = jnp.broadcast_to(jnp.arange(rows, dtype=F32)[:, None], (rows, GRID_W)).reshape(-1)
    col = jnp.broadcast_to(jnp.arange(GRID_W, dtype=F32)[None, :], (rows, GRID_W)).reshape(-1)
    axis_dim = DIFF_HEAD_DIM // 2
    inv_freq = ROPE_BASE ** (-jnp.arange(0, axis_dim, 2, dtype=F32) / axis_dim)
    ang_r = row[:, None] * inv_freq
    ang_c = col[:, None] * inv_freq
    cos64 = jnp.concatenate([jnp.cos(ang_r), jnp.cos(ang_r), jnp.cos(ang_c), jnp.cos(ang_c)], axis=1)
    sin64 = jnp.concatenate([-jnp.sin(ang_r), jnp.sin(ang_r), -jnp.sin(ang_c), jnp.sin(ang_c)], axis=1)
    reps = LANES // DIFF_HEAD_DIM
    return (jnp.tile(cos64, (1, reps)), jnp.tile(sin64, (1, reps)), cos64.T, sin64.T)


def kernel(x, c, ctx, c_ctx, ada_w, ada_b, norm_g, ffn_w_gate, ffn_w_up, ffn_w_down, w_in, w_out,
           hyena_conv_w, hyena_conv_b, filt_w1, filt_b1, filt_w_inner, filt_b_inner, filt_sin_freq,
           filt_w_out, hyena_bias, diff_lambda, diff_subln_g, final_g):
    nb, n, d = x.shape
    layer = 0
    hy_end = HYENA_N_PROJ * D_HYENA
    q_end = hy_end + D_DIFF
    k_end = q_end + D_DIFF

    cc = jnp.concatenate([c, c_ctx[None, :], jnp.zeros((8 - nb - 1, d), F32)], axis=0)
    mod = _modulation(cc, ada_w[layer], ada_b[layer])
    mod_lat = [mod[:nb, j * d:(j + 1) * d][:, None, :] for j in range(N_ADA)]
    mod_ctx = [mod[nb:nb + 1, j * d:(j + 1) * d][:, None, :] for j in range(N_ADA)]

    def ffn_weights(i):
        return (ffn_w_gate[layer, i].astype(BF16), ffn_w_up[layer, i].astype(BF16),
                ffn_w_down[layer, i].astype(BF16))

    ffn0 = ffn_weights(0)
    h_lat = _ffn_half_step(x, norm_g[layer, 0], mod_lat[0], mod_lat[1], mod_lat[2], *ffn0)
    h_ctx = _ffn_half_step(ctx, norm_g[layer, 0], mod_ctx[0], mod_ctx[1], mod_ctx[2], *ffn0)

    w = w_in[layer]
    wzk = jnp.concatenate([w[:, :hy_end], w[:, q_end:k_end]], axis=1).astype(BF16)
    wqv = jnp.concatenate([w[:, hy_end:q_end], w[:, k_end:]], axis=1).T.astype(BF16)
    ck, sk, cq, sq = _rope_tables(n)
    u_hy, x0_hy, k_lat, qT, vT4 = _input_projection(
        h_lat, norm_g[layer, 1], mod_lat[3], mod_lat[4], wzk, wqv, ck, sk, cq, sq,
        hyena_conv_w[layer], hyena_conv_b[layer])
    k_ctx, vT_ctx = _context_projection(h_ctx, norm_g[layer, 1], mod_ctx[3], mod_ctx[4],
                                        w[:, q_end:k_end].astype(BF16), w[:, k_end:].T.astype(BF16))

    y_diff = _diff_attention(qT, k_lat, vT4, k_ctx, vT_ctx, diff_lambda[layer].astype(F32),
                             diff_subln_g[layer])
    y_hy = _hyena_mixer(u_hy, x0_hy, filt_w1[layer], filt_b1[layer], filt_w_inner[layer],
                        filt_b_inner[layer], filt_sin_freq[layer], filt_w_out[layer], hyena_bias[layer])

    wo = w_out[layer].astype(BF16)
    mixer = (y_hy, y_diff, wo[:D_HYENA], wo[D_HYENA:], mod_lat[5])
    return _ffn_half_step(h_lat, norm_g[layer, 2], mod_lat[6], mod_lat[7], mod_lat[8],
                          *ffn_weights(1), mixer=mixer, final_g=final_g)
```

```python
import functools
import math

import numpy as np
import jax
import jax.numpy as jnp
from jax import lax
from jax.experimental import pallas as pl
from jax.experimental.pallas import tpu as pltpu

F32 = jnp.float32
BF16 = jnp.bfloat16

N_ADA = 9
FFN_HIDDEN = 2816
D_HYENA = 512
HYENA_N_PROJ = 3
FILTER_EMB_BANDS = 16
FILTER_HIDDEN = 64
FILTER_EMB_ROWS = 40
FILTER_INNER = 2
DECAY_TARGET = 1e-2
FAST_DECAY_PCT = 0.3
SLOW_DECAY_PCT = 1.5
N_DIFF_HEADS = 4
DIFF_HEAD_DIM = 64
D_DIFF = N_DIFF_HEADS * 2 * DIFF_HEAD_DIM
GRID_W = 64
ROPE_BASE = 10000.0
RMS_EPS = 1e-6
SUBLN_EPS = 1e-5
LAM_INIT = 0.8 - 0.6 * math.exp(-0.3 * 0)

LANES = 128
BF16_ROWS = 16
V7X_VMEM_BYTES = 64 * 2**20
VMEM_LIMIT = 56 * 2**20

FFN_TM = 512
FFN_TF = 256
PROJ_TM = 512
ATT_TK = 512
ATT_STEPS = 2
ATT_TQ = 2048
FFT_M2 = 128
FFT_TN = 4096
FFT_KC = 8
FILT_TL = 1024
NEG_BIG = -1e30

LOG2E = 1.4426950408889634


def _cparams(*sem):
    return pltpu.CompilerParams(dimension_semantics=sem, vmem_limit_bytes=VMEM_LIMIT)


def _dot(a, b):
    return jnp.dot(a, b, preferred_element_type=F32)


def _dot3(a, b):
    a_hi = a.astype(BF16)
    a_lo = (a - a_hi.astype(F32)).astype(BF16)
    b_hi = b.astype(BF16)
    b_lo = (b - b_hi.astype(F32)).astype(BF16)
    return _dot(a_hi, b_hi) + (_dot(a_lo, b_hi) + _dot(a_hi, b_lo))


def _silu(x):
    return x * (1.0 / (1.0 + jnp.exp(-x)))


def _adaln(s, g, shift, scale):
    ms = jnp.mean(s * s, axis=-1, keepdims=True)
    return (s * lax.rsqrt(ms + RMS_EPS) * g) * (1.0 + scale) + shift


def _mod_kernel(c_ref, w_ref, b_ref, o_ref):
    o_ref[...] = _dot3(_silu(c_ref[...]), w_ref[...]) + b_ref[...]


def _modulation(cc, ada_w, ada_b):
    rows, d = cc.shape
    n = ada_w.shape[1]
    tn = 768
    return pl.pallas_call(
        _mod_kernel,
        grid=(n // tn,),
        in_specs=[pl.BlockSpec((rows, d), lambda j: (0, 0)),
                  pl.BlockSpec((d, tn), lambda j: (0, j)),
                  pl.BlockSpec((1, tn), lambda j: (0, j))],
        out_specs=pl.BlockSpec((rows, tn), lambda j: (0, j)),
        out_shape=jax.ShapeDtypeStruct((rows, n), F32),
        compiler_params=_cparams("parallel"),
        name="modulation",
    )(cc, ada_w, ada_b.reshape(1, n))


def _mod_spec(arr, nb):
    d = arr.shape[-1]
    if arr.shape[0] == nb:
        return pl.BlockSpec((1, 1, d), lambda b, i: (b, 0, 0))
    return pl.BlockSpec((1, 1, d), lambda b, i: (0, 0, 0))


def _const_spec(arr):
    nd = arr.ndim
    return pl.BlockSpec(arr.shape, lambda b, i: (0,) * nd)


def _ffn_kernel(*refs, n_chunks, tf, mix, final):
    refs = list(refs)
    o_ref = refs.pop()
    s_ref, g_ref, sh_ref, sc_ref, gt_ref, wg_ref, wu_ref, wd_ref = refs[:8]
    rest = refs[8:]
    s = s_ref[0]
    if mix:
        cv_ref, x0_ref, u_ref, hb_ref, yd_ref, woa_ref, wob_ref, gm_ref = rest[:8]
        rest = rest[8:]
        yh = x0_ref[0].astype(F32) * (cv_ref[0].astype(F32) + u_ref[0].astype(F32) * hb_ref[...])
        s = s + gm_ref[0] * (_dot(yh.astype(BF16), woa_ref[...]) + _dot(yd_ref[0], wob_ref[...]))
    ub = _adaln(s, g_ref[...], sh_ref[0], sc_ref[0]).astype(BF16)
    acc = jnp.zeros(s.shape, F32)
    for f in range(n_chunks):
        cols = slice(f * tf, (f + 1) * tf)
        gate = _dot(ub, wg_ref[:, cols])
        up = _dot(ub, wu_ref[:, cols])
        a = (_silu(gate) * up).astype(BF16)
        acc = acc + _dot(a, wd_ref[cols, :])
    out = s + (0.5 * gt_ref[0]) * acc
    if final:
        ms = jnp.mean(out * out, axis=-1, keepdims=True)
        out = out * lax.rsqrt(ms + RMS_EPS) * rest[0][...]
    o_ref[0] = out


def _ffn_half_step(s, g, shift, scale, gate, weights, mixer=None, final_g=None):
    nb, t, d = s.shape
    tm = min(FFN_TM, t)
    wg, wu, wd, widx = weights

    def wspec(a):
        return pl.BlockSpec((None, None) + a.shape[2:], lambda b, i: tuple(widx) + (0, 0))

    args = [s, g.reshape(1, d), shift, scale, gate, wg, wu, wd]
    in_specs = [pl.BlockSpec((1, tm, d), lambda b, i: (b, i, 0)),
                _const_spec(args[1]),
                _mod_spec(shift, nb), _mod_spec(scale, nb), _mod_spec(gate, nb),
                wspec(wg), wspec(wu), wspec(wd)]
    if mixer is not None:
        cv, x0, u, hb, yd, woa, wob, gm = mixer
        hb = hb.reshape(1, -1).astype(F32)
        args += [cv, x0, u, hb, yd, woa, wob, gm]
        tile = lambda a: pl.BlockSpec((1, tm, a.shape[2]), lambda b, i: (b, i, 0))
        in_specs += [tile(cv), tile(x0), tile(u), _const_spec(hb), tile(yd),
                     _const_spec(woa), _const_spec(wob), _mod_spec(gm, nb)]
    if final_g is not None:
        args.append(final_g.reshape(1, d))
        in_specs.append(_const_spec(args[-1]))
    kern = functools.partial(_ffn_kernel, n_chunks=wg.shape[3] // FFN_TF, tf=FFN_TF,
                             mix=mixer is not None, final=final_g is not None)
    return pl.pallas_call(
        kern,
        grid=(nb, t // tm),
        in_specs=in_specs,
        out_specs=pl.BlockSpec((1, tm, d), lambda b, i: (b, i, 0)),
        out_shape=jax.ShapeDtypeStruct((nb, t, d), F32),
        compiler_params=_cparams("parallel", "parallel"),
        name="ffn_mix_final" if mixer is not None else "ffn",
    )(*args)


def _rope_partner(x, axis):
    n = x.shape[axis]
    idx = lax.broadcasted_iota(jnp.int32, x.shape, axis)
    first_half = (idx & 16) == 0
    return jnp.where(first_half, pltpu.roll(x, n - 16, axis), pltpu.roll(x, 16, axis))


def _inproj_kernel(s_ref, sp_ref, sn_ref, g_ref, sh_ref, sc_ref, wzk_ref, wqv_ref, ck_ref, sk_ref,
                   cq_ref, sq_ref, cw_ref, cb_ref, u_ref, x0_ref, k_ref, qT_ref, vT_ref, *, q_scale):
    i = pl.program_id(1)
    last = pl.num_programs(1) - 1
    tm = s_ref.shape[1]
    rows = jnp.concatenate([s_ref[0], sp_ref[0], sn_ref[0]], axis=0)
    ub_ext = _adaln(rows, g_ref[...], sh_ref[0], sc_ref[0]).astype(BF16)
    ub = ub_ext[:tm]
    zk = _dot(ub_ext, wzk_ref[...])
    hy = HYENA_N_PROJ * D_HYENA
    z = zk[:tm, :hy]
    prev = jnp.where(i == 0, 0.0, zk[tm + 7:tm + 8, :hy])
    nxt = jnp.where(i == last, 0.0, zk[tm + 8:tm + 9, :hy])
    row = lax.broadcasted_iota(jnp.int32, z.shape, 0)
    z_dn = jnp.where(row == 0, prev, pltpu.roll(z, 1, 0))
    z_up = jnp.where(row == tm - 1, nxt, pltpu.roll(z, tm - 1, 0))
    y = z_dn * cw_ref[0:1] + z * cw_ref[1:2] + z_up * cw_ref[2:3] + cb_ref[...]
    c = D_HYENA
    x0_ref[0] = y[:, :c].astype(BF16)
    u_ref[0] = (y[:, c:2 * c] * y[:, 2 * c:]).astype(BF16)
    k = zk[:tm, hy:]
    reps = D_DIFF // LANES
    ck = jnp.concatenate([ck_ref[...]] * reps, axis=1)
    sk = jnp.concatenate([sk_ref[...]] * reps, axis=1)
    k_ref[0] = (k * ck + _rope_partner(k, 1) * sk).astype(BF16)
    qv = lax.dot_general(wqv_ref[...], ub, (((1,), (1,)), ((), ())), preferred_element_type=F32)
    q = qv[:D_DIFF]
    groups = D_DIFF // DIFF_HEAD_DIM
    cq = jnp.broadcast_to(cq_ref[...][None], (groups, DIFF_HEAD_DIM, tm)).reshape(D_DIFF, tm)
    sq = jnp.broadcast_to(sq_ref[...][None], (groups, DIFF_HEAD_DIM, tm)).reshape(D_DIFF, tm)
    q = (q * cq + _rope_partner(q, 0) * sq) * q_scale
    qT_ref[0] = q.astype(BF16)
    v = qv[D_DIFF:].astype(BF16)
    tk = vT_ref.shape[3]
    for j in range(vT_ref.shape[1]):
        vT_ref[0, j] = v[:, j * tk:(j + 1) * tk]


def _input_projection(h, g, shift, scale, wzk, wqv, ck, sk, cq, sq, conv_w, conv_b):
    nb, n, d = h.shape
    tm = PROJ_TM
    r, nblk8 = tm // 8, n // 8
    c3 = conv_w.shape[1]
    q_scale = DIFF_HEAD_DIM ** -0.5 * LOG2E
    args = [h, h, h, g.reshape(1, d), shift, scale, wzk, wqv, ck, sk, cq, sq, conv_w, conv_b.reshape(1, c3)]
    in_specs = [pl.BlockSpec((1, tm, d), lambda b, i: (b, i, 0)),
                pl.BlockSpec((1, 8, d), lambda b, i: (b, jnp.maximum(i * r - 1, 0), 0)),
                pl.BlockSpec((1, 8, d), lambda b, i: (b, jnp.minimum((i + 1) * r, nblk8 - 1), 0)),
                _const_spec(args[3]), _mod_spec(shift, nb), _mod_spec(scale, nb),
                _const_spec(wzk), _const_spec(wqv),
                pl.BlockSpec((tm, LANES), lambda b, i: (i, 0)),
                pl.BlockSpec((tm, LANES), lambda b, i: (i, 0)),
                pl.BlockSpec((DIFF_HEAD_DIM, tm), lambda b, i: (0, i)),
                pl.BlockSpec((DIFF_HEAD_DIM, tm), lambda b, i: (0, i)),
                _const_spec(conv_w), _const_spec(args[-1])]
    out_shape = (jax.ShapeDtypeStruct((nb, n, D_HYENA), BF16),
                 jax.ShapeDtypeStruct((nb, n, D_HYENA), BF16),
                 jax.ShapeDtypeStruct((nb, n, D_DIFF), BF16),
                 jax.ShapeDtypeStruct((nb, D_DIFF, n), BF16),
                 jax.ShapeDtypeStruct((nb, n // ATT_TK, D_DIFF, ATT_TK), BF16))
    out_specs = (pl.BlockSpec((1, tm, D_HYENA), lambda b, i: (b, i, 0)),
                 pl.BlockSpec((1, tm, D_HYENA), lambda b, i: (b, i, 0)),
                 pl.BlockSpec((1, tm, D_DIFF), lambda b, i: (b, i, 0)),
                 pl.BlockSpec((1, D_DIFF, tm), lambda b, i: (b, 0, i)),
                 pl.BlockSpec((1, tm // ATT_TK, D_DIFF, ATT_TK), lambda b, i: (b, i, 0, 0)))
    return pl.pallas_call(
        functools.partial(_inproj_kernel, q_scale=q_scale),
        grid=(nb, n // tm),
        in_specs=in_specs, out_specs=out_specs, out_shape=out_shape,
        compiler_params=_cparams("parallel", "parallel"),
        name="input_projection",
    )(*args)


def _inproj_ctx_kernel(s_ref, g_ref, sh_ref, sc_ref, wk_ref, wv_ref, k_ref, vT_ref):
    ub = _adaln(s_ref[0], g_ref[...], sh_ref[0], sc_ref[0]).astype(BF16)
    k_ref[0] = _dot(ub, wk_ref[...]).astype(BF16)
    vT_ref[0] = lax.dot_general(wv_ref[...], ub, (((1,), (1,)), ((), ())),
                                preferred_element_type=F32).astype(BF16)


def _context_projection(h, g, shift, scale, wk, wvT):
    nb, c, d = h.shape
    args = [h, g.reshape(1, d), shift, scale, wk, wvT]
    in_specs = [pl.BlockSpec((1, c, d), lambda b, i: (b, 0, 0)),
                _const_spec(args[1]), _mod_spec(shift, nb), _mod_spec(scale, nb),
                _const_spec(wk), _const_spec(wvT)]
    return pl.pallas_call(
        _inproj_ctx_kernel,
        grid=(nb, 1),
        in_specs=in_specs,
        out_specs=(pl.BlockSpec((1, c, D_DIFF), lambda b, i: (b, 0, 0)),
                   pl.BlockSpec((1, D_DIFF, c), lambda b, i: (b, 0, 0))),
        out_shape=(jax.ShapeDtypeStruct((nb, c, D_DIFF), BF16),
                   jax.ShapeDtypeStruct((nb, D_DIFF, c), BF16)),
        compiler_params=_cparams("parallel", "parallel"),
        name="context_projection",
    )(*args)


def _attn_kernel(qT_ref, k_ref, vT_ref, kc_ref, vTc_ref, lp_ref, sg_ref, o_ref,
                 acc0_ref, acc1_ref, qa_ref, qb_ref, *sp_refs, n_chunks, tk):
    d = DIFF_HEAD_DIM
    s_refs = (sp_refs[0:2], sp_refs[2:4])
    p_refs = (sp_refs[4:6], sp_refs[6:8])
    q = qT_ref[0]
    tq = q.shape[1]
    row = lax.broadcasted_iota(jnp.int32, q.shape, 0)
    zero = jnp.zeros_like(q)
    qa_ref[...] = jnp.where(row < d, q, zero)
    qb_ref[...] = jnp.where(row < d, zero, q)
    acc0_ref[...] = jnp.zeros_like(acc0_ref)
    acc1_ref[...] = jnp.zeros_like(acc1_ref)
    q_refs = (qa_ref, qb_ref)
    acc_refs = (acc0_ref, acc1_ref)

    def with_ones(vc):
        r = lax.broadcasted_iota(jnp.int32, (BF16_ROWS, vc.shape[1]), 0)
        return jnp.concatenate([vc, jnp.where(r == 0, 1.0, 0.0).astype(BF16)], axis=0)

    def scores(kc, slot):
        cmax = []
        for h in range(2):
            s = _dot(kc, q_refs[h][...])
            s_refs[slot][h][...] = s
            cmax.append(jnp.max(s, axis=0, keepdims=True))
        return tuple(cmax)

    def probs(s, cmax, m):
        m_new = jnp.maximum(m, cmax)
        return jnp.exp2((s - m_new).astype(BF16)), m_new, jnp.exp2(m - m_new)

    def softmax_stage(slot, cmax, stats):
        out = []
        for h in range(2):
            p, m, alpha = probs(s_refs[slot][h][...], cmax[h], stats[h][0])
            p_refs[slot][h][...] = p
            out.append((m, alpha))
        return tuple(out)

    def values(vc, slot, stats):
        for h in range(2):
            acc_refs[h][...] = stats[h][1] * acc_refs[h][...] + _dot(vc, p_refs[slot][h][...])

    def lat_keys(t):
        return k_ref[0, pl.ds(pl.multiple_of(t * tk, tk), tk), :]

    def step(t, slot, cmax_other, stats, with_scores=True):
        cmax = scores(lat_keys(t + 2), slot) if with_scores else None
        values(with_ones(vT_ref[0, t]), slot, stats)
        new_stats = softmax_stage(1 - slot, cmax_other, stats)
        return cmax, new_stats

    neg = jnp.full((1, tq), NEG_BIG, F32)
    one = jnp.ones((1, tq), F32)
    cmax0 = scores(lat_keys(0), 0)
    cmax1 = scores(lat_keys(1), 1)
    carry = (cmax1, softmax_stage(0, cmax0, ((neg, one), (neg, one))))

    def body(i, carry):
        for j in range(ATT_STEPS):
            carry = step(ATT_STEPS * i + j, j % 2, *carry)
        return carry

    n_loop = (n_chunks - 2) // ATT_STEPS
    carry = lax.fori_loop(0, n_loop, body, carry)
    for t in range(n_loop * ATT_STEPS, n_chunks - 1):
        carry = step(t, t % 2, *carry, with_scores=t + 2 < n_chunks)
    stats = carry[1]
    values(with_ones(vT_ref[0, n_chunks - 1]), (n_chunks - 1) % 2, stats)

    kc = kc_ref[0]
    vc = with_ones(vTc_ref[0])
    for h in range(2):
        s = _dot(kc, q_refs[h][...])
        p, _, alpha = probs(s, jnp.max(s, axis=0, keepdims=True), stats[h][0])
        acc_refs[h][...] = alpha * acc_refs[h][...] + _dot(vc, p)

    lp = lp_ref[...]
    lam = (jnp.exp(jnp.sum(lp[0:1] * lp[1:2], axis=1, keepdims=True))
           - jnp.exp(jnp.sum(lp[2:3] * lp[3:4], axis=1, keepdims=True)) + LAM_INIT)
    hd = 2 * d
    o0 = acc0_ref[:hd, :] / acc0_ref[hd:hd + 1, :]
    o1 = acc1_ref[:hd, :] / acc1_ref[hd:hd + 1, :]
    oT = o0 - lam * o1
    ms = jnp.mean(oT * oT, axis=0, keepdims=True)
    oT = oT * lax.rsqrt(ms + SUBLN_EPS)
    o_ref[0] = ((oT.T * sg_ref[...]) * (1.0 - LAM_INIT)).astype(BF16)


def _diff_attention(qT, k, vT4, k_ctx, vT_ctx, lam_params, subln_g):
    nb, n, _ = k.shape
    hd = 2 * DIFF_HEAD_DIM
    n_chunks, tk = vT4.shape[1], vT4.shape[3]
    c = k_ctx.shape[1]
    tq = ATT_TQ
    in_specs = [pl.BlockSpec((1, hd, tq), lambda b, h, i: (b, h, i)),
                pl.BlockSpec((1, n, hd), lambda b, h, i: (b, 0, h)),
                pl.BlockSpec((1, n_chunks, hd, tk), lambda b, h, i: (b, 0, h, 0)),
                pl.BlockSpec((1, c, hd), lambda b, h, i: (b, 0, h)),
                pl.BlockSpec((1, hd, c), lambda b, h, i: (b, h, 0)),
                pl.BlockSpec(lam_params.shape, lambda b, h, i: (0, 0)),
                pl.BlockSpec((1, hd), lambda b, h, i: (0, 0))]
    return pl.pallas_call(
        functools.partial(_attn_kernel, n_chunks=n_chunks, tk=tk),
        grid=(nb, N_DIFF_HEADS, n // tq),
        in_specs=in_specs,
        out_specs=pl.BlockSpec((1, tq, hd), lambda b, h, i: (b, i, h)),
        out_shape=jax.ShapeDtypeStruct((nb, n, D_DIFF), BF16),
        scratch_shapes=([pltpu.VMEM((hd + BF16_ROWS, tq), F32)] * 2 + [pltpu.VMEM((hd, tq), BF16)] * 2
                        + [pltpu.VMEM((tk, tq), F32)] * 4 + [pltpu.VMEM((tk, tq), BF16)] * 4),
        compiler_params=_cparams("parallel", "parallel", "arbitrary"),
        name="diff_attention",
    )(qT, k, vT4, k_ctx, vT_ctx, lam_params, subln_g.reshape(1, hd))


def _filter_kernel(tc_ref, tr_ref, om_ref, band_ref, sgn_ref, ph_ref, w1_ref, b1_ref, wi_ref, bi_ref,
                   fr_ref, wo_ref, dl_ref, h_ref):
    tr = tr_ref[...]
    rows, tl = FILTER_EMB_ROWS, tr.shape[1]
    r = lax.broadcasted_iota(jnp.int32, (rows, tl), 0)
    trig = jnp.sin(sgn_ref[...] * (band_ref[...] * om_ref[...]) + ph_ref[...])
    emb = jnp.where(r == 0, tr, jnp.where(r <= 2 * FILTER_EMB_BANDS, trig, 0.0))
    fr = fr_ref[...]
    h = jnp.sin(fr * (_dot3(w1_ref[...], emb) + b1_ref[...]))
    for i in range(FILTER_INNER):
        h = jnp.sin(fr * (_dot3(wi_ref[i], h) + bi_ref[i]))
    h = _dot3(h.T, wo_ref[...]) * jnp.exp(-tc_ref[...] * dl_ref[...])
    c = D_HYENA
    h_ref[0] = h[:, :c].astype(BF16)
    first = (pl.program_id(0) == 0) & (lax.broadcasted_iota(jnp.int32, (tl, c), 0) == 0)
    h_ref[1] = jnp.where(first, 0.0, h[:, c:]).astype(BF16)


def _hyena_filter(n, w1, b1, w_inner, b_inner, freq, w_out):
    tl = min(FILT_TL, n)
    t = jnp.linspace(0.0, 1.0, n, dtype=F32)
    omega = ((2.0 * math.pi / n) * jnp.arange(n, dtype=F32))[None, :]
    nbands, rows, fh = FILTER_EMB_BANDS, FILTER_EMB_ROWS, FILTER_HIDDEN
    bands = jnp.linspace(1e-4, nbands - 1, nbands, dtype=F32)
    pad = jnp.zeros((rows - 1 - 2 * nbands,), F32)
    zero1, ones, zeros = jnp.zeros((1,), F32), jnp.ones((nbands,), F32), jnp.zeros((nbands,), F32)
    band_col = jnp.concatenate([zero1, bands, bands, pad])[:, None]
    sgn_col = jnp.concatenate([zero1, ones, -ones, pad])[:, None]
    ph_col = jnp.concatenate([zero1, ones * (0.5 * math.pi), zeros, pad])[:, None]
    w1t = jnp.concatenate([w1.astype(F32), jnp.zeros((rows - w1.shape[0], fh), F32)], axis=0).T
    min_decay = math.log(DECAY_TARGET) / SLOW_DECAY_PCT
    max_decay = math.log(DECAY_TARGET) / FAST_DECAY_PCT
    deltas = jnp.abs(jnp.linspace(min_decay, max_decay, D_HYENA, dtype=F32))
    dl = jnp.concatenate([deltas, deltas])[None, :]
    args = [t[:, None], t[None, :], omega, band_col, sgn_col, ph_col, w1t, b1.reshape(fh, 1),
            jnp.swapaxes(w_inner, 1, 2), b_inner.reshape(FILTER_INNER, fh, 1), freq.reshape(fh, 1), w_out, dl]

    def cs(a):
        nd = a.ndim
        return pl.BlockSpec(a.shape, lambda i: (0,) * nd)

    in_specs = [pl.BlockSpec((tl, 1), lambda i: (i, 0)), pl.BlockSpec((1, tl), lambda i: (0, i)),
                pl.BlockSpec((1, tl), lambda i: (0, i))]
    in_specs += [cs(a) for a in args[3:]]
    return pl.pallas_call(
        _filter_kernel,
        grid=(n // tl,),
        in_specs=in_specs,
        out_specs=pl.BlockSpec((2, tl, D_HYENA), lambda i: (0, i, 0)),
        out_shape=jax.ShapeDtypeStruct((2, n, D_HYENA), BF16),
        compiler_params=_cparams("parallel"),
        name="hyena_filter",
    )(*args)


def _dft_tables(n):
    m2 = FFT_M2
    m1 = 2 * n // m2
    m = 2 * n
    mk = m1 // 2 + FFT_KC
    k1 = np.arange(mk, dtype=np.float64)[:, None]
    n1 = np.arange(m1 // 2, dtype=np.float64)[None, :]
    ang1 = 2.0 * np.pi * k1 * n1 / m1
    f1 = np.concatenate([np.cos(ang1), -np.sin(ang1)], axis=0)
    wgt = np.where((k1 == 0) | (k1 == m1 // 2), 1.0, np.where(k1 < m1 // 2, 2.0, 0.0))
    finv = np.concatenate([(wgt * np.cos(ang1)).T, -(wgt * np.sin(ang1)).T], axis=1)
    a = np.arange(m2, dtype=np.float64)
    ang2 = 2.0 * np.pi * a[:, None] * a[None, :] / m2
    angt = 2.0 * np.pi * k1 * a[None, :] / m
    return dict(
        m1=m1, mk=mk,
        f1=jnp.asarray(f1, F32), finv=jnp.asarray(finv, F32),
        f2r=jnp.asarray(np.cos(ang2), F32), f2i=jnp.asarray(-np.sin(ang2), F32),
        twr=jnp.asarray(np.cos(angt)[:, None, :], F32), twi=jnp.asarray(-np.sin(angt)[:, None, :], F32))


def _dft_a_kernel(x_ref, f_ref, o_ref):
    o_ref[0] = _dot(f_ref[...].astype(BF16), x_ref[0].astype(BF16)).astype(BF16)


def _dft_stage_a(x, f1):
    g, r, w = x.shape
    tn = min(FFT_TN, w)
    rows = f1.shape[0]
    return pl.pallas_call(
        _dft_a_kernel,
        grid=(g, w // tn),
        in_specs=[pl.BlockSpec((1, r, tn), lambda b, j: (b, 0, j)),
                  pl.BlockSpec(f1.shape, lambda b, j: (0, 0))],
        out_specs=pl.BlockSpec((1, rows, tn), lambda b, j: (b, 0, j)),
        out_shape=jax.ShapeDtypeStruct((g, rows, w), BF16),
        compiler_params=_cparams("parallel", "parallel"),
        name="dft_stage_a",
    )(x, f1)


def _twiddled_dft(f2r, f2i, twr, twi):
    gr = f2r * twr - f2i * twi
    gi = f2r * twi + f2i * twr
    return jnp.concatenate([jnp.concatenate([gr, -gi], axis=1),
                            jnp.concatenate([gi, gr], axis=1)], axis=0)


def _filter_spectrum_kernel(a_ref, f2r_ref, f2i_ref, twr_ref, twi_ref, k_ref, *, kc):
    m2 = FFT_M2
    for j in range(kc):
        g = _twiddled_dft(f2r_ref[...], f2i_ref[...], twr_ref[j], twi_ref[j]).astype(BF16)
        xf = _dot(g, a_ref[0, :, j].reshape(2 * m2, -1))
        xb = _dot(g, a_ref[1, :, j].reshape(2 * m2, -1))
        k_ref[0, j] = (xf[:m2] + xb[:m2]).astype(BF16)
        k_ref[1, j] = (xf[m2:] - xb[m2:]).astype(BF16)


def _filter_spectrum(a5, tabs):
    _, _, m1, m2, c = a5.shape
    kc = FFT_KC
    in_specs = [pl.BlockSpec((2, 2, kc, m2, c), lambda i: (0, 0, i, 0, 0)),
                pl.BlockSpec((m2, m2), lambda i: (0, 0)), pl.BlockSpec((m2, m2), lambda i: (0, 0)),
                pl.BlockSpec((kc, 1, m2), lambda i: (i, 0, 0)), pl.BlockSpec((kc, 1, m2), lambda i: (i, 0, 0))]
    return pl.pallas_call(
        functools.partial(_filter_spectrum_kernel, kc=kc),
        grid=(m1 // kc,),
        in_specs=in_specs,
        out_specs=pl.BlockSpec((2, kc, m2, c), lambda i: (0, i, 0, 0)),
        out_shape=jax.ShapeDtypeStruct((2, m1, m2, c), BF16),
        compiler_params=_cparams("parallel"),
        name="filter_spectrum",
    )(a5, tabs["f2r"], tabs["f2i"], tabs["twr"], tabs["twi"])


def _spectral_kernel(a_ref, kf_ref, f2r_ref, f2i_ref, twr_ref, twi_ref, b_ref, *, kc, nb):
    m2 = FFT_M2

    def forward(j):
        g32 = _twiddled_dft(f2r_ref[...], f2i_ref[...], twr_ref[j], twi_ref[j])
        g = g32.astype(BF16)
        xs = [_dot(g, a_ref[b, :, j].reshape(2 * m2, -1)) for b in range(nb)]
        return g32.T.astype(BF16), xs

    def inverse(j, gt, xs):
        kr = kf_ref[0, j].astype(F32)
        ki = kf_ref[1, j].astype(F32)
        for b in range(nb):
            xr, xi = xs[b][:m2], xs[b][m2:]
            y = jnp.concatenate([xr * kr - xi * ki, xr * ki + xi * kr], axis=0).astype(BF16)
            z = _dot(gt, y)
            b_ref[b, :, j] = z.reshape(2, m2, -1).astype(BF16)

    cur = forward(0)
    for j in range(1, kc):
        nxt = forward(j)
        inverse(j - 1, *cur)
        cur = nxt
    inverse(kc - 1, *cur)


def _spectral_multiply(a5, kf, tabs):
    nb, _, m1, m2, c = a5.shape
    kc = FFT_KC
    in_specs = [pl.BlockSpec((nb, 2, kc, m2, c), lambda i: (0, 0, i, 0, 0)),
                pl.BlockSpec((2, kc, m2, c), lambda i: (0, i, 0, 0)),
                pl.BlockSpec((m2, m2), lambda i: (0, 0)), pl.BlockSpec((m2, m2), lambda i: (0, 0)),
                pl.BlockSpec((kc, 1, m2), lambda i: (i, 0, 0)), pl.BlockSpec((kc, 1, m2), lambda i: (i, 0, 0))]
    return pl.pallas_call(
        functools.partial(_spectral_kernel, kc=kc, nb=nb),
        grid=(m1 // kc,),
        in_specs=in_specs,
        out_specs=pl.BlockSpec((nb, 2, kc, m2, c), lambda i: (0, 0, i, 0, 0)),
        out_shape=jax.ShapeDtypeStruct((nb, 2, m1, m2, c), BF16),
        compiler_params=_cparams("parallel"),
        name="spectral_multiply",
    )(a5, kf, tabs["f2r"], tabs["f2i"], tabs["twr"], tabs["twi"])


def _idft_a_kernel(b_ref, f_ref, y_ref, *, inv_m):
    y_ref[0] = (_dot(f_ref[...].astype(BF16), b_ref[0]) * inv_m).astype(BF16)


def _idft_stage_a(bm, finv, inv_m):
    g, rows, w = bm.shape
    r = finv.shape[0]
    tn = min(FFT_TN, w)
    return pl.pallas_call(
        functools.partial(_idft_a_kernel, inv_m=inv_m),
        grid=(g, w // tn),
        in_specs=[pl.BlockSpec((1, rows, tn), lambda b, j: (b, 0, j)),
                  pl.BlockSpec(finv.shape, lambda b, j: (0, 0))],
        out_specs=pl.BlockSpec((1, r, tn), lambda b, j: (b, 0, j)),
        out_shape=jax.ShapeDtypeStruct((g, r, w), BF16),
        compiler_params=_cparams("parallel", "parallel"),
        name="idft_stage_a",
    )(bm, finv)


def _hyena_long_conv(u, w1, b1, w_inner, b_inner, freq, w_out):
    nb, n, c = u.shape
    tabs = _dft_tables(n)
    m1, mk, m2 = tabs["m1"], tabs["mk"], FFT_M2
    w = m2 * c
    h = _hyena_filter(n, w1, b1, w_inner, b_inner, freq, w_out)
    ah = _dft_stage_a(h.reshape(2, m1 // 2, w), tabs["f1"])
    kf = _filter_spectrum(ah.reshape(2, 2, mk, m2, c), tabs)
    au = _dft_stage_a(u.reshape(nb, m1 // 2, w), tabs["f1"])
    bm = _spectral_multiply(au.reshape(nb, 2, mk, m2, c), kf, tabs)
    y = _idft_stage_a(bm.reshape(nb, 2 * mk, w), tabs["finv"], 1.0 / (2 * n))
    return y.reshape(nb, n, c)


def _rope_tables(n):
    rows = n // GRID_W
    row = jnp.broadcast_to(jnp.arange(rows, dtype=F32)[:, None], (rows, GRID_W)).reshape(-1)
    col = jnp.broadcast_to(jnp.arange(GRID_W, dtype=F32)[None, :], (rows, GRID_W)).reshape(-1)
    axis_dim = DIFF_HEAD_DIM // 2
    inv_freq = ROPE_BASE ** (-jnp.arange(0, axis_dim, 2, dtype=F32) / axis_dim)
    ang_r = row[:, None] * inv_freq
    ang_c = col[:, None] * inv_freq
    cos64 = jnp.concatenate([jnp.cos(ang_r), jnp.cos(ang_r), jnp.cos(ang_c), jnp.cos(ang_c)], axis=1)
    sin64 = jnp.concatenate([-jnp.sin(ang_r), jnp.sin(ang_r), -jnp.sin(ang_c), jnp.sin(ang_c)], axis=1)
    reps = LANES // DIFF_HEAD_DIM
    return (jnp.tile(cos64, (1, reps)), jnp.tile(sin64, (1, reps)), cos64.T, sin64.T)


def kernel(x, c, ctx, c_ctx, ada_w, ada_b, norm_g, ffn_w_gate, ffn_w_up, ffn_w_down, w_in, w_out,
           hyena_conv_w, hyena_conv_b, filt_w1, filt_b1, filt_w_inner, filt_b_inner, filt_sin_freq,
           filt_w_out, hyena_bias, diff_lambda, diff_subln_g, final_g):
    nb, n, d = x.shape
    layer = 0
    hy_end = HYENA_N_PROJ * D_HYENA
    q_end = hy_end + D_DIFF
    k_end = q_end + D_DIFF

    cc = jnp.concatenate([c, c_ctx[None, :], jnp.zeros((8 - nb - 1, d), F32)], axis=0)
    mod = _modulation(cc, ada_w[layer], ada_b[layer])
    mod_lat = [mod[:nb, j * d:(j + 1) * d][:, None, :] for j in range(N_ADA)]
    mod_ctx = [mod[nb:nb + 1, j * d:(j + 1) * d][:, None, :] for j in range(N_ADA)]

    ffn_w = (ffn_w_gate.astype(BF16), ffn_w_up.astype(BF16), ffn_w_down.astype(BF16))
    ffn0 = ffn_w + ((layer, 0),)
    h_lat = _ffn_half_step(x, norm_g[layer, 0], mod_lat[0], mod_lat[1], mod_lat[2], ffn0)
    h_ctx = _ffn_half_step(ctx, norm_g[layer, 0], mod_ctx[0], mod_ctx[1], mod_ctx[2], ffn0)

    w = w_in[layer]
    wzk = jnp.concatenate([w[:, :hy_end], w[:, q_end:k_end]], axis=1).astype(BF16)
    wqv = jnp.concatenate([w[:, hy_end:q_end], w[:, k_end:]], axis=1).T.astype(BF16)
    ck, sk, cq, sq = _rope_tables(n)
    u_hy, x0_hy, k_lat, qT, vT4 = _input_projection(
        h_lat, norm_g[layer, 1], mod_lat[3], mod_lat[4], wzk, wqv, ck, sk, cq, sq,
        hyena_conv_w[layer], hyena_conv_b[layer])
    k_ctx, vT_ctx = _context_projection(h_ctx, norm_g[layer, 1], mod_ctx[3], mod_ctx[4],
                                        w[:, q_end:k_end].astype(BF16), w[:, k_end:].T.astype(BF16))

    y_diff = _diff_attention(qT, k_lat, vT4, k_ctx, vT_ctx, diff_lambda[layer].astype(F32),
                             diff_subln_g[layer])
    conv_hy = _hyena_long_conv(u_hy, filt_w1[layer], filt_b1[layer], filt_w_inner[layer],
                               filt_b_inner[layer], filt_sin_freq[layer], filt_w_out[layer])

    wo = w_out[layer].astype(BF16)
    mixer = (conv_hy, x0_hy, u_hy, hyena_bias[layer], y_diff, wo[:D_HYENA], wo[D_HYENA:], mod_lat[5])
    return _ffn_half_step(h_lat, norm_g[layer, 2], mod_lat[6], mod_lat[7], mod_lat[8],
                          ffn_w + ((layer, 1),), mixer=mixer, final_g=final_g)
```

```python
import functools
import math

import numpy as np
import jax
import jax.numpy as jnp
from jax import lax
from jax.experimental import pallas as pl
from jax.experimental.pallas import tpu as pltpu

F32 = jnp.float32
BF16 = jnp.bfloat16

N_ADA = 9
FFN_HIDDEN = 2816
D_HYENA = 512
HYENA_N_PROJ = 3
FILTER_EMB_BANDS = 16
FILTER_HIDDEN = 64
FILTER_EMB_ROWS = 40
FILTER_INNER = 2
DECAY_TARGET = 1e-2
FAST_DECAY_PCT = 0.3
SLOW_DECAY_PCT = 1.5
N_DIFF_HEADS = 4
DIFF_HEAD_DIM = 64
D_DIFF = N_DIFF_HEADS * 2 * DIFF_HEAD_DIM
GRID_W = 64
ROPE_BASE = 10000.0
RMS_EPS = 1e-6
SUBLN_EPS = 1e-5
LAM_INIT = 0.8 - 0.6 * math.exp(-0.3 * 0)

LANES = 128
BF16_ROWS = 16
V7X_VMEM_BYTES = 64 * 2**20
VMEM_LIMIT = 56 * 2**20

FFN_TM = 512
FFN_TF = 256
PROJ_TM = 512
ATT_TK = 512
ATT_STEPS = 2
ATT_TQ = 2048
FFT_M2 = 128
FFT_TN = 4096
FFT_KC = 8
FILT_TL = 1024
NEG_BIG = -1e30

LOG2E = 1.4426950408889634


def _cparams(*sem):
    return pltpu.CompilerParams(dimension_semantics=sem, vmem_limit_bytes=VMEM_LIMIT)


def _dot(a, b):
    return jnp.dot(a, b, preferred_element_type=F32)


def _dot3(a, b):
    a_hi = a.astype(BF16)
    a_lo = (a - a_hi.astype(F32)).astype(BF16)
    b_hi = b.astype(BF16)
    b_lo = (b - b_hi.astype(F32)).astype(BF16)
    return _dot(a_hi, b_hi) + (_dot(a_lo, b_hi) + _dot(a_hi, b_lo))


def _silu(x):
    return x * (1.0 / (1.0 + jnp.exp(-x)))


def _adaln(s, g, shift, scale):
    ms = jnp.mean(s * s, axis=-1, keepdims=True)
    return (s * lax.rsqrt(ms + RMS_EPS) * g) * (1.0 + scale) + shift


def _mod_kernel(c_ref, w_ref, b_ref, o_ref):
    o_ref[...] = _dot3(_silu(c_ref[...]), w_ref[...]) + b_ref[...]


def _modulation(cc, ada_w, ada_b):
    rows, d = cc.shape
    n = ada_w.shape[1]
    tn = 768
    return pl.pallas_call(
        _mod_kernel,
        grid=(n // tn,),
        in_specs=[pl.BlockSpec((rows, d), lambda j: (0, 0)),
                  pl.BlockSpec((d, tn), lambda j: (0, j)),
                  pl.BlockSpec((1, tn), lambda j: (0, j))],
        out_specs=pl.BlockSpec((rows, tn), lambda j: (0, j)),
        out_shape=jax.ShapeDtypeStruct((rows, n), F32),
        compiler_params=_cparams("parallel"),
        name="modulation",
    )(cc, ada_w, ada_b.reshape(1, n))


def _mod_spec(arr, nb):
    d = arr.shape[-1]
    if arr.shape[0] == nb:
        return pl.BlockSpec((1, 1, d), lambda b, i: (b, 0, 0))
    return pl.BlockSpec((1, 1, d), lambda b, i: (0, 0, 0))


def _const_spec(arr):
    nd = arr.ndim
    return pl.BlockSpec(arr.shape, lambda b, i: (0,) * nd)


def _ffn_kernel(*refs, n_chunks, tf, mix, final):
    refs = list(refs)
    o_ref = refs.pop()
    s_ref, g_ref, sh_ref, sc_ref, gt_ref, wg_ref, wu_ref, wd_ref = refs[:8]
    rest = refs[8:]
    s = s_ref[0]
    if mix:
        cv_ref, x0_ref, u_ref, hb_ref, yd_ref, woa_ref, wob_ref, gm_ref = rest[:8]
        rest = rest[8:]
        yh = x0_ref[0].astype(F32) * (cv_ref[0].astype(F32) + u_ref[0].astype(F32) * hb_ref[...])
        s = s + gm_ref[0] * (_dot(yh.astype(BF16), woa_ref[...]) + _dot(yd_ref[0], wob_ref[...]))
    ub = _adaln(s, g_ref[...], sh_ref[0], sc_ref[0]).astype(BF16)
    acc = jnp.zeros(s.shape, F32)
    for f in range(n_chunks):
        cols = slice(f * tf, (f + 1) * tf)
        gate = _dot(ub, wg_ref[:, cols])
        up = _dot(ub, wu_ref[:, cols])
        a = (_silu(gate) * up).astype(BF16)
        acc = acc + _dot(a, wd_ref[cols, :])
    out = s + (0.5 * gt_ref[0]) * acc
    if final:
        ms = jnp.mean(out * out, axis=-1, keepdims=True)
        out = out * lax.rsqrt(ms + RMS_EPS) * rest[0][...]
    o_ref[0] = out


def _ffn_half_step(s, g, shift, scale, gate, weights, mixer=None, final_g=None):
    nb, t, d = s.shape
    tm = min(FFN_TM, t)
    wg, wu, wd, widx = weights

    def wspec(a):
        return pl.BlockSpec((None, None) + a.shape[2:], lambda b, i: tuple(widx) + (0, 0))

    args = [s, g.reshape(1, d), shift, scale, gate, wg, wu, wd]
    in_specs = [pl.BlockSpec((1, tm, d), lambda b, i: (b, i, 0)),
                _const_spec(args[1]),
                _mod_spec(shift, nb), _mod_spec(scale, nb), _mod_spec(gate, nb),
                wspec(wg), wspec(wu), wspec(wd)]
    if mixer is not None:
        cv, x0, u, hb, yd, woa, wob, gm = mixer
        hb = hb.reshape(1, -1).astype(F32)
        args += [cv, x0, u, hb, yd, woa, wob, gm]
        tile = lambda a: pl.BlockSpec((1, tm, a.shape[2]), lambda b, i: (b, i, 0))
        in_specs += [tile(cv), tile(x0), tile(u), _const_spec(hb), tile(yd),
                     _const_spec(woa), _const_spec(wob), _mod_spec(gm, nb)]
    if final_g is not None:
        args.append(final_g.reshape(1, d))
        in_specs.append(_const_spec(args[-1]))
    kern = functools.partial(_ffn_kernel, n_chunks=wg.shape[3] // FFN_TF, tf=FFN_TF,
                             mix=mixer is not None, final=final_g is not None)
    return pl.pallas_call(
        kern,
        grid=(nb, t // tm),
        in_specs=in_specs,
        out_specs=pl.BlockSpec((1, tm, d), lambda b, i: (b, i, 0)),
        out_shape=jax.ShapeDtypeStruct((nb, t, d), F32),
        compiler_params=_cparams("parallel", "parallel"),
        name="ffn_mix_final" if mixer is not None else "ffn",
    )(*args)


def _rope_partner(x, axis):
    n = x.shape[axis]
    idx = lax.broadcasted_iota(jnp.int32, x.shape, axis)
    first_half = (idx & 16) == 0
    return jnp.where(first_half, pltpu.roll(x, n - 16, axis), pltpu.roll(x, 16, axis))


def _inproj_kernel(s_ref, sp_ref, sn_ref, g_ref, sh_ref, sc_ref, wzk_ref, wqv_ref, ck_ref, sk_ref,
                   cq_ref, sq_ref, cw_ref, cb_ref, u_ref, x0_ref, k_ref, qT_ref, vT_ref, *, q_scale):
    i = pl.program_id(1)
    last = pl.num_programs(1) - 1
    tm = s_ref.shape[1]
    rows = jnp.concatenate([s_ref[0], sp_ref[0], sn_ref[0]], axis=0)
    ub_ext = _adaln(rows, g_ref[...], sh_ref[0], sc_ref[0]).astype(BF16)
    ub = ub_ext[:tm]
    zk = _dot(ub_ext, wzk_ref[...])
    hy = HYENA_N_PROJ * D_HYENA
    z = zk[:tm, :hy]
    prev = jnp.where(i == 0, 0.0, zk[tm + 7:tm + 8, :hy])
    nxt = jnp.where(i == last, 0.0, zk[tm + 8:tm + 9, :hy])
    row = lax.broadcasted_iota(jnp.int32, z.shape, 0)
    z_dn = jnp.where(row == 0, prev, pltpu.roll(z, 1, 0))
    z_up = jnp.where(row == tm - 1, nxt, pltpu.roll(z, tm - 1, 0))
    y = z_dn * cw_ref[0:1] + z * cw_ref[1:2] + z_up * cw_ref[2:3] + cb_ref[...]
    c = D_HYENA
    x0_ref[0] = y[:, :c].astype(BF16)
    u_ref[0] = (y[:, c:2 * c] * y[:, 2 * c:]).astype(BF16)
    k = zk[:tm, hy:]
    reps = D_DIFF // LANES
    ck = jnp.concatenate([ck_ref[...]] * reps, axis=1)
    sk = jnp.concatenate([sk_ref[...]] * reps, axis=1)
    k_ref[0] = (k * ck + _rope_partner(k, 1) * sk).astype(BF16)
    qv = lax.dot_general(wqv_ref[...], ub, (((1,), (1,)), ((), ())), preferred_element_type=F32)
    q = qv[:D_DIFF]
    groups = D_DIFF // DIFF_HEAD_DIM
    cq = jnp.broadcast_to(cq_ref[...][None], (groups, DIFF_HEAD_DIM, tm)).reshape(D_DIFF, tm)
    sq = jnp.broadcast_to(sq_ref[...][None], (groups, DIFF_HEAD_DIM, tm)).reshape(D_DIFF, tm)
    q = (q * cq + _rope_partner(q, 0) * sq) * q_scale
    qT_ref[0] = q.astype(BF16)
    v = qv[D_DIFF:].astype(BF16)
    tk = vT_ref.shape[3]
    for j in range(vT_ref.shape[1]):
        vT_ref[0, j] = v[:, j * tk:(j + 1) * tk]


def _input_projection(h, g, shift, scale, wzk, wqv, ck, sk, cq, sq, conv_w, conv_b):
    nb, n, d = h.shape
    tm = PROJ_TM
    r, nblk8 = tm // 8, n // 8
    c3 = conv_w.shape[1]
    q_scale = DIFF_HEAD_DIM ** -0.5 * LOG2E
    args = [h, h, h, g.reshape(1, d), shift, scale, wzk, wqv, ck, sk, cq, sq, conv_w, conv_b.reshape(1, c3)]
    in_specs = [pl.BlockSpec((1, tm, d), lambda b, i: (b, i, 0)),
                pl.BlockSpec((1, 8, d), lambda b, i: (b, jnp.maximum(i * r - 1, 0), 0)),
                pl.BlockSpec((1, 8, d), lambda b, i: (b, jnp.minimum((i + 1) * r, nblk8 - 1), 0)),
                _const_spec(args[3]), _mod_spec(shift, nb), _mod_spec(scale, nb),
                _const_spec(wzk), _const_spec(wqv),
                pl.BlockSpec((tm, LANES), lambda b, i: (i, 0)),
                pl.BlockSpec((tm, LANES), lambda b, i: (i, 0)),
                pl.BlockSpec((DIFF_HEAD_DIM, tm), lambda b, i: (0, i)),
                pl.BlockSpec((DIFF_HEAD_DIM, tm), lambda b, i: (0, i)),
                _const_spec(conv_w), _const_spec(args[-1])]
    out_shape = (jax.ShapeDtypeStruct((nb, n, D_HYENA), BF16),
                 jax.ShapeDtypeStruct((nb, n, D_HYENA), BF16),
                 jax.ShapeDtypeStruct((nb, n, D_DIFF), BF16),
                 jax.ShapeDtypeStruct((nb, D_DIFF, n), BF16),
                 jax.ShapeDtypeStruct((nb, n // ATT_TK, D_DIFF, ATT_TK), BF16))
    out_specs = (pl.BlockSpec((1, tm, D_HYENA), lambda b, i: (b, i, 0)),
                 pl.BlockSpec((1, tm, D_HYENA), lambda b, i: (b, i, 0)),
                 pl.BlockSpec((1, tm, D_DIFF), lambda b, i: (b, i, 0)),
                 pl.BlockSpec((1, D_DIFF, tm), lambda b, i: (b, 0, i)),
                 pl.BlockSpec((1, tm // ATT_TK, D_DIFF, ATT_TK), lambda b, i: (b, i, 0, 0)))
    return pl.pallas_call(
        functools.partial(_inproj_kernel, q_scale=q_scale),
        grid=(nb, n // tm),
        in_specs=in_specs, out_specs=out_specs, out_shape=out_shape,
        compiler_params=_cparams("parallel", "parallel"),
        name="input_projection",
    )(*args)


def _ctx_stream_kernel(s_ref, g_ref, sh_ref, sc_ref, gt_ref, wg_ref, wu_ref, wd_ref,
                       g1_ref, sh1_ref, sc1_ref, wk_ref, wv_ref, k_ref, vT_ref, *, n_chunks, tf):
    s = s_ref[0]
    ub = _adaln(s, g_ref[...], sh_ref[0], sc_ref[0]).astype(BF16)
    acc = jnp.zeros(s.shape, F32)
    for f in range(n_chunks):
        cols = slice(f * tf, (f + 1) * tf)
        a = (_silu(_dot(ub, wg_ref[:, cols])) * _dot(ub, wu_ref[:, cols])).astype(BF16)
        acc = acc + _dot(a, wd_ref[cols, :])
    h = s + (0.5 * gt_ref[0]) * acc
    ub1 = _adaln(h, g1_ref[...], sh1_ref[0], sc1_ref[0]).astype(BF16)
    k_ref[0] = _dot(ub1, wk_ref[...]).astype(BF16)
    vT_ref[0] = lax.dot_general(wv_ref[...], ub1, (((1,), (1,)), ((), ())),
                                preferred_element_type=F32).astype(BF16)


def _context_stream(s, g0, shift0, scale0, gate0, weights, g1, shift1, scale1, wk, wvT):
    nb, c, d = s.shape
    wg, wu, wd, widx = weights

    def wspec(a):
        return pl.BlockSpec((None, None) + a.shape[2:], lambda b, i: tuple(widx) + (0, 0))

    args = [s, g0.reshape(1, d), shift0, scale0, gate0, wg, wu, wd,
            g1.reshape(1, d), shift1, scale1, wk, wvT]
    in_specs = [pl.BlockSpec((1, c, d), lambda b, i: (b, 0, 0)), _const_spec(args[1]),
                _mod_spec(shift0, nb), _mod_spec(scale0, nb), _mod_spec(gate0, nb),
                wspec(wg), wspec(wu), wspec(wd), _const_spec(args[8]),
                _mod_spec(shift1, nb), _mod_spec(scale1, nb), _const_spec(wk), _const_spec(wvT)]
    return pl.pallas_call(
        functools.partial(_ctx_stream_kernel, n_chunks=wg.shape[3] // FFN_TF, tf=FFN_TF),
        grid=(nb, 1),
        in_specs=in_specs,
        out_specs=(pl.BlockSpec((1, c, D_DIFF), lambda b, i: (b, 0, 0)),
                   pl.BlockSpec((1, D_DIFF, c), lambda b, i: (b, 0, 0))),
        out_shape=(jax.ShapeDtypeStruct((nb, c, D_DIFF), BF16),
                   jax.ShapeDtypeStruct((nb, D_DIFF, c), BF16)),
        compiler_params=_cparams("parallel", "parallel"),
        name="context_stream",
    )(*args)


def _attn_kernel(qT_ref, k_ref, vT_ref, kc_ref, vTc_ref, lp_ref, sg_ref, o_ref,
                 acc0_ref, acc1_ref, qa_ref, qb_ref, *sp_refs, n_chunks, tk):
    d = DIFF_HEAD_DIM
    s_refs = (sp_refs[0:2], sp_refs[2:4])
    p_refs = (sp_refs[4:6], sp_refs[6:8])
    q = qT_ref[0]
    tq = q.shape[1]
    row = lax.broadcasted_iota(jnp.int32, q.shape, 0)
    zero = jnp.zeros_like(q)
    qa_ref[...] = jnp.where(row < d, q, zero)
    qb_ref[...] = jnp.where(row < d, zero, q)
    acc0_ref[...] = jnp.zeros_like(acc0_ref)
    acc1_ref[...] = jnp.zeros_like(acc1_ref)
    q_refs = (qa_ref, qb_ref)
    acc_refs = (acc0_ref, acc1_ref)

    def with_ones(vc):
        r = lax.broadcasted_iota(jnp.int32, (BF16_ROWS, vc.shape[1]), 0)
        return jnp.concatenate([vc, jnp.where(r == 0, 1.0, 0.0).astype(BF16)], axis=0)

    def scores(kc, slot):
        cmax = []
        for h in range(2):
            s = _dot(kc, q_refs[h][...])
            s_refs[slot][h][...] = s
            cmax.append(jnp.max(s, axis=0, keepdims=True))
        return tuple(cmax)

    def probs(s, cmax, m):
        m_new = jnp.maximum(m, cmax)
        return jnp.exp2((s - m_new).astype(BF16)), m_new, jnp.exp2(m - m_new)

    def softmax_stage(slot, cmax, stats):
        out = []
        for h in range(2):
            p, m, alpha = probs(s_refs[slot][h][...], cmax[h], stats[h][0])
            p_refs[slot][h][...] = p
            out.append((m, alpha))
        return tuple(out)

    def values(vc, slot, stats):
        for h in range(2):
            acc_refs[h][...] = stats[h][1] * acc_refs[h][...] + _dot(vc, p_refs[slot][h][...])

    def lat_keys(t):
        return k_ref[0, pl.ds(pl.multiple_of(t * tk, tk), tk), :]

    def step(t, slot, cmax_other, stats, with_scores=True):
        cmax = scores(lat_keys(t + 2), slot) if with_scores else None
        values(with_ones(vT_ref[0, t]), slot, stats)
        new_stats = softmax_stage(1 - slot, cmax_other, stats)
        return cmax, new_stats

    neg = jnp.full((1, tq), NEG_BIG, F32)
    one = jnp.ones((1, tq), F32)
    cmax0 = scores(lat_keys(0), 0)
    cmax1 = scores(lat_keys(1), 1)
    carry = (cmax1, softmax_stage(0, cmax0, ((neg, one), (neg, one))))

    def body(i, carry):
        for j in range(ATT_STEPS):
            carry = step(ATT_STEPS * i + j, j % 2, *carry)
        return carry

    n_loop = (n_chunks - 2) // ATT_STEPS
    carry = lax.fori_loop(0, n_loop, body, carry)
    for t in range(n_loop * ATT_STEPS, n_chunks - 1):
        carry = step(t, t % 2, *carry, with_scores=t + 2 < n_chunks)
    stats = carry[1]
    values(with_ones(vT_ref[0, n_chunks - 1]), (n_chunks - 1) % 2, stats)

    kc = kc_ref[0]
    vc = with_ones(vTc_ref[0])
    for h in range(2):
        s = _dot(kc, q_refs[h][...])
        p, _, alpha = probs(s, jnp.max(s, axis=0, keepdims=True), stats[h][0])
        acc_refs[h][...] = alpha * acc_refs[h][...] + _dot(vc, p)

    lp = lp_ref[...]
    lam = (jnp.exp(jnp.sum(lp[0:1] * lp[1:2], axis=1, keepdims=True))
           - jnp.exp(jnp.sum(lp[2:3] * lp[3:4], axis=1, keepdims=True)) + LAM_INIT)
    hd = 2 * d
    o0 = acc0_ref[:hd, :] / acc0_ref[hd:hd + 1, :]
    o1 = acc1_ref[:hd, :] / acc1_ref[hd:hd + 1, :]
    oT = o0 - lam * o1
    ms = jnp.mean(oT * oT, axis=0, keepdims=True)
    oT = oT * lax.rsqrt(ms + SUBLN_EPS)
    o_ref[0] = ((oT.T * sg_ref[...]) * (1.0 - LAM_INIT)).astype(BF16)


def _diff_attention(qT, k, vT4, k_ctx, vT_ctx, lam_params, subln_g):
    nb, n, _ = k.shape
    hd = 2 * DIFF_HEAD_DIM
    n_chunks, tk = vT4.shape[1], vT4.shape[3]
    c = k_ctx.shape[1]
    tq = ATT_TQ
    in_specs = [pl.BlockSpec((1, hd, tq), lambda b, h, i: (b, h, i)),
                pl.BlockSpec((1, n, hd), lambda b, h, i: (b, 0, h)),
                pl.BlockSpec((1, n_chunks, hd, tk), lambda b, h, i: (b, 0, h, 0)),
                pl.BlockSpec((1, c, hd), lambda b, h, i: (b, 0, h)),
                pl.BlockSpec((1, hd, c), lambda b, h, i: (b, h, 0)),
                pl.BlockSpec(lam_params.shape, lambda b, h, i: (0, 0)),
                pl.BlockSpec((1, hd), lambda b, h, i: (0, 0))]
    return pl.pallas_call(
        functools.partial(_attn_kernel, n_chunks=n_chunks, tk=tk),
        grid=(nb, N_DIFF_HEADS, n // tq),
        in_specs=in_specs,
        out_specs=pl.BlockSpec((1, tq, hd), lambda b, h, i: (b, i, h)),
        out_shape=jax.ShapeDtypeStruct((nb, n, D_DIFF), BF16),
        scratch_shapes=([pltpu.VMEM((hd + BF16_ROWS, tq), F32)] * 2 + [pltpu.VMEM((hd, tq), BF16)] * 2
                        + [pltpu.VMEM((tk, tq), F32)] * 4 + [pltpu.VMEM((tk, tq), BF16)] * 4),
        compiler_params=_cparams("parallel", "parallel", "arbitrary"),
        name="diff_attention",
    )(qT, k, vT4, k_ctx, vT_ctx, lam_params, subln_g.reshape(1, hd))


def _filter_kernel(tc_ref, tr_ref, om_ref, band_ref, sgn_ref, ph_ref, w1_ref, b1_ref, wi_ref, bi_ref,
                   fr_ref, wo_ref, dl_ref, h_ref):
    tr = tr_ref[...]
    rows, tl = FILTER_EMB_ROWS, tr.shape[1]
    r = lax.broadcasted_iota(jnp.int32, (rows, tl), 0)
    trig = jnp.sin(sgn_ref[...] * (band_ref[...] * om_ref[...]) + ph_ref[...])
    emb = jnp.where(r == 0, tr, jnp.where(r <= 2 * FILTER_EMB_BANDS, trig, 0.0))
    fr = fr_ref[...]
    h = jnp.sin(fr * (_dot3(w1_ref[...], emb) + b1_ref[...]))
    for i in range(FILTER_INNER):
        h = jnp.sin(fr * (_dot3(wi_ref[i], h) + bi_ref[i]))
    h = _dot3(h.T, wo_ref[...]) * jnp.exp(-tc_ref[...] * dl_ref[...])
    c = D_HYENA
    h_ref[0] = h[:, :c].astype(BF16)
    first = (pl.program_id(0) == 0) & (lax.broadcasted_iota(jnp.int32, (tl, c), 0) == 0)
    h_ref[1] = jnp.where(first, 0.0, h[:, c:]).astype(BF16)


def _hyena_filter(n, w1, b1, w_inner, b_inner, freq, w_out):
    tl = min(FILT_TL, n)
    t = jnp.linspace(0.0, 1.0, n, dtype=F32)
    omega = ((2.0 * math.pi / n) * jnp.arange(n, dtype=F32))[None, :]
    nbands, rows, fh = FILTER_EMB_BANDS, FILTER_EMB_ROWS, FILTER_HIDDEN
    bands = jnp.linspace(1e-4, nbands - 1, nbands, dtype=F32)
    pad = jnp.zeros((rows - 1 - 2 * nbands,), F32)
    zero1, ones, zeros = jnp.zeros((1,), F32), jnp.ones((nbands,), F32), jnp.zeros((nbands,), F32)
    band_col = jnp.concatenate([zero1, bands, bands, pad])[:, None]
    sgn_col = jnp.concatenate([zero1, ones, -ones, pad])[:, None]
    ph_col = jnp.concatenate([zero1, ones * (0.5 * math.pi), zeros, pad])[:, None]
    w1t = jnp.concatenate([w1.astype(F32), jnp.zeros((rows - w1.shape[0], fh), F32)], axis=0).T
    min_decay = math.log(DECAY_TARGET) / SLOW_DECAY_PCT
    max_decay = math.log(DECAY_TARGET) / FAST_DECAY_PCT
    deltas = jnp.abs(jnp.linspace(min_decay, max_decay, D_HYENA, dtype=F32))
    dl = jnp.concatenate([deltas, deltas])[None, :]
    args = [t[:, None], t[None, :], omega, band_col, sgn_col, ph_col, w1t, b1.reshape(fh, 1),
            jnp.swapaxes(w_inner, 1, 2), b_inner.reshape(FILTER_INNER, fh, 1), freq.reshape(fh, 1), w_out, dl]

    def cs(a):
        nd = a.ndim
        return pl.BlockSpec(a.shape, lambda i: (0,) * nd)

    in_specs = [pl.BlockSpec((tl, 1), lambda i: (i, 0)), pl.BlockSpec((1, tl), lambda i: (0, i)),
                pl.BlockSpec((1, tl), lambda i: (0, i))]
    in_specs += [cs(a) for a in args[3:]]
    return pl.pallas_call(
        _filter_kernel,
        grid=(n // tl,),
        in_specs=in_specs,
        out_specs=pl.BlockSpec((2, tl, D_HYENA), lambda i: (0, i, 0)),
        out_shape=jax.ShapeDtypeStruct((2, n, D_HYENA), BF16),
        compiler_params=_cparams("parallel"),
        name="hyena_filter",
    )(*args)


def _dft_tables(n):
    m2 = FFT_M2
    m1 = 2 * n // m2
    m = 2 * n
    mk = m1 // 2 + FFT_KC
    k1 = np.arange(mk, dtype=np.float64)[:, None]
    n1 = np.arange(m1 // 2, dtype=np.float64)[None, :]
    ang1 = 2.0 * np.pi * k1 * n1 / m1
    f1 = np.concatenate([np.cos(ang1), -np.sin(ang1)], axis=0)
    wgt = np.where((k1 == 0) | (k1 == m1 // 2), 1.0, np.where(k1 < m1 // 2, 2.0, 0.0))
    finv = np.concatenate([(wgt * np.cos(ang1)).T, -(wgt * np.sin(ang1)).T], axis=1)
    a = np.arange(m2, dtype=np.float64)
    ang2 = 2.0 * np.pi * a[:, None] * a[None, :] / m2
    angt = 2.0 * np.pi * k1 * a[None, :] / m
    return dict(
        m1=m1, mk=mk,
        f1=jnp.asarray(f1, F32), finv=jnp.asarray(finv, F32),
        f2r=jnp.asarray(np.cos(ang2), F32), f2i=jnp.asarray(-np.sin(ang2), F32),
        twr=jnp.asarray(np.cos(angt)[:, None, :], F32), twi=jnp.asarray(-np.sin(angt)[:, None, :], F32))


def _dft_a_kernel(x_ref, f_ref, o_ref):
    o_ref[0] = _dot(f_ref[...].astype(BF16), x_ref[0].astype(BF16)).astype(BF16)


def _dft_stage_a(x, f1):
    g, r, w = x.shape
    tn = min(FFT_TN, w)
    rows = f1.shape[0]
    return pl.pallas_call(
        _dft_a_kernel,
        grid=(g, w // tn),
        in_specs=[pl.BlockSpec((1, r, tn), lambda b, j: (b, 0, j)),
                  pl.BlockSpec(f1.shape, lambda b, j: (0, 0))],
        out_specs=pl.BlockSpec((1, rows, tn), lambda b, j: (b, 0, j)),
        out_shape=jax.ShapeDtypeStruct((g, rows, w), BF16),
        compiler_params=_cparams("parallel", "parallel"),
        name="dft_stage_a",
    )(x, f1)


def _twiddled_dft(f2r, f2i, twr, twi):
    gr = f2r * twr - f2i * twi
    gi = f2r * twi + f2i * twr
    return jnp.concatenate([jnp.concatenate([gr, -gi], axis=1),
                            jnp.concatenate([gi, gr], axis=1)], axis=0)


def _filter_spectrum_kernel(a_ref, f2r_ref, f2i_ref, twr_ref, twi_ref, k_ref, *, kc):
    m2 = FFT_M2
    for j in range(kc):
        g = _twiddled_dft(f2r_ref[...], f2i_ref[...], twr_ref[j], twi_ref[j]).astype(BF16)
        xf = _dot(g, a_ref[0, :, j].reshape(2 * m2, -1))
        xb = _dot(g, a_ref[1, :, j].reshape(2 * m2, -1))
        k_ref[0, j] = (xf[:m2] + xb[:m2]).astype(BF16)
        k_ref[1, j] = (xf[m2:] - xb[m2:]).astype(BF16)


def _filter_spectrum(a5, tabs):
    _, _, m1, m2, c = a5.shape
    kc = FFT_KC
    in_specs = [pl.BlockSpec((2, 2, kc, m2, c), lambda i: (0, 0, i, 0, 0)),
                pl.BlockSpec((m2, m2), lambda i: (0, 0)), pl.BlockSpec((m2, m2), lambda i: (0, 0)),
                pl.BlockSpec((kc, 1, m2), lambda i: (i, 0, 0)), pl.BlockSpec((kc, 1, m2), lambda i: (i, 0, 0))]
    return pl.pallas_call(
        functools.partial(_filter_spectrum_kernel, kc=kc),
        grid=(m1 // kc,),
        in_specs=in_specs,
        out_specs=pl.BlockSpec((2, kc, m2, c), lambda i: (0, i, 0, 0)),
        out_shape=jax.ShapeDtypeStruct((2, m1, m2, c), BF16),
        compiler_params=_cparams("parallel"),
        name="filter_spectrum",
    )(a5, tabs["f2r"], tabs["f2i"], tabs["twr"], tabs["twi"])


def _spectral_kernel(a_ref, kf_ref, f2r_ref, f2i_ref, twr_ref, twi_ref, b_ref, *, kc, nb):
    m2 = FFT_M2

    def forward(j):
        g32 = _twiddled_dft(f2r_ref[...], f2i_ref[...], twr_ref[j], twi_ref[j])
        g = g32.astype(BF16)
        xs = [_dot(g, a_ref[b, :, j].reshape(2 * m2, -1)) for b in range(nb)]
        return g32.T.astype(BF16), xs

    def inverse(j, gt, xs):
        kr = kf_ref[0, j].astype(F32)
        ki = kf_ref[1, j].astype(F32)
        for b in range(nb):
            xr, xi = xs[b][:m2], xs[b][m2:]
            y = jnp.concatenate([xr * kr - xi * ki, xr * ki + xi * kr], axis=0).astype(BF16)
            z = _dot(gt, y)
            b_ref[b, :, j] = z.reshape(2, m2, -1).astype(BF16)

    cur = forward(0)
    for j in range(1, kc):
        nxt = forward(j)
        inverse(j - 1, *cur)
        cur = nxt
    inverse(kc - 1, *cur)


def _spectral_multiply(a5, kf, tabs):
    nb, _, m1, m2, c = a5.shape
    kc = FFT_KC
    in_specs = [pl.BlockSpec((nb, 2, kc, m2, c), lambda i: (0, 0, i, 0, 0)),
                pl.BlockSpec((2, kc, m2, c), lambda i: (0, i, 0, 0)),
                pl.BlockSpec((m2, m2), lambda i: (0, 0)), pl.BlockSpec((m2, m2), lambda i: (0, 0)),
                pl.BlockSpec((kc, 1, m2), lambda i: (i, 0, 0)), pl.BlockSpec((kc, 1, m2), lambda i: (i, 0, 0))]
    return pl.pallas_call(
        functools.partial(_spectral_kernel, kc=kc, nb=nb),
        grid=(m1 // kc,),
        in_specs=in_specs,
        out_specs=pl.BlockSpec((nb, 2, kc, m2, c), lambda i: (0, 0, i, 0, 0)),
        out_shape=jax.ShapeDtypeStruct((nb, 2, m1, m2, c), BF16),
        compiler_params=_cparams("parallel"),
        name="spectral_multiply",
    )(a5, kf, tabs["f2r"], tabs["f2i"], tabs["twr"], tabs["twi"])


def _idft_a_kernel(b_ref, f_ref, y_ref, *, inv_m):
    y_ref[0] = (_dot(f_ref[...].astype(BF16), b_ref[0]) * inv_m).astype(BF16)


def _idft_stage_a(bm, finv, inv_m):
    g, rows, w = bm.shape
    r = finv.shape[0]
    tn = min(FFT_TN, w)
    return pl.pallas_call(
        functools.partial(_idft_a_kernel, inv_m=inv_m),
        grid=(g, w // tn),
        in_specs=[pl.BlockSpec((1, rows, tn), lambda b, j: (b, 0, j)),
                  pl.BlockSpec(finv.shape, lambda b, j: (0, 0))],
        out_specs=pl.BlockSpec((1, r, tn), lambda b, j: (b, 0, j)),
        out_shape=jax.ShapeDtypeStruct((g, r, w), BF16),
        compiler_params=_cparams("parallel", "parallel"),
        name="idft_stage_a",
    )(bm, finv)


def _hyena_long_conv(u, w1, b1, w_inner, b_inner, freq, w_out):
    nb, n, c = u.shape
    tabs = _dft_tables(n)
    m1, mk, m2 = tabs["m1"], tabs["mk"], FFT_M2
    w = m2 * c
    h = _hyena_filter(n, w1, b1, w_inner, b_inner, freq, w_out)
    ah = _dft_stage_a(h.reshape(2, m1 // 2, w), tabs["f1"])
    kf = _filter_spectrum(ah.reshape(2, 2, mk, m2, c), tabs)
    au = _dft_stage_a(u.reshape(nb, m1 // 2, w), tabs["f1"])
    bm = _spectral_multiply(au.reshape(nb, 2, mk, m2, c), kf, tabs)
    y = _idft_stage_a(bm.reshape(nb, 2 * mk, w), tabs["finv"], 1.0 / (2 * n))
    return y.reshape(nb, n, c)


def _rope_tables(n):
    rows = n // GRID_W
    row = jnp.broadcast_to(jnp.arange(rows, dtype=F32)[:, None], (rows, GRID_W)).reshape(-1)
    col = jnp.broadcast_to(jnp.arange(GRID_W, dtype=F32)[None, :], (rows, GRID_W)).reshape(-1)
    axis_dim = DIFF_HEAD_DIM // 2
    inv_freq = ROPE_BASE ** (-jnp.arange(0, axis_dim, 2, dtype=F32) / axis_dim)
    ang_r = row[:, None] * inv_freq
    ang_c = col[:, None] * inv_freq
    cos64 = jnp.concatenate([jnp.cos(ang_r), jnp.cos(ang_r), jnp.cos(ang_c), jnp.cos(ang_c)], axis=1)
    sin64 = jnp.concatenate([-jnp.sin(ang_r), jnp.sin(ang_r), -jnp.sin(ang_c), jnp.sin(ang_c)], axis=1)
    reps = LANES // DIFF_HEAD_DIM
    return (jnp.tile(cos64, (1, reps)), jnp.tile(sin64, (1, reps)), cos64.T, sin64.T)


def kernel(x, c, ctx, c_ctx, ada_w, ada_b, norm_g, ffn_w_gate, ffn_w_up, ffn_w_down, w_in, w_out,
           hyena_conv_w, hyena_conv_b, filt_w1, filt_b1, filt_w_inner, filt_b_inner, filt_sin_freq,
           filt_w_out, hyena_bias, diff_lambda, diff_subln_g, final_g):
    nb, n, d = x.shape
    layer = 0
    hy_end = HYENA_N_PROJ * D_HYENA
    q_end = hy_end + D_DIFF
    k_end = q_end + D_DIFF

    cc = jnp.concatenate([c, c_ctx[None, :], jnp.zeros((8 - nb - 1, d), F32)], axis=0)
    mod = _modulation(cc, ada_w[layer], ada_b[layer])
    mod_lat = [mod[:nb, j * d:(j + 1) * d][:, None, :] for j in range(N_ADA)]
    mod_ctx = [mod[nb:nb + 1, j * d:(j + 1) * d][:, None, :] for j in range(N_ADA)]

    ffn_w = (ffn_w_gate.astype(BF16), ffn_w_up.astype(BF16), ffn_w_down.astype(BF16))
    ffn0 = ffn_w + ((layer, 0),)
    h_lat = _ffn_half_step(x, norm_g[layer, 0], mod_lat[0], mod_lat[1], mod_lat[2], ffn0)

    w = w_in[layer]
    wzk = jnp.concatenate([w[:, :hy_end], w[:, q_end:k_end]], axis=1).astype(BF16)
    wqv = jnp.concatenate([w[:, hy_end:q_end], w[:, k_end:]], axis=1).T.astype(BF16)
    ck, sk, cq, sq = _rope_tables(n)
    u_hy, x0_hy, k_lat, qT, vT4 = _input_projection(
        h_lat, norm_g[layer, 1], mod_lat[3], mod_lat[4], wzk, wqv, ck, sk, cq, sq,
        hyena_conv_w[layer], hyena_conv_b[layer])
    k_ctx, vT_ctx = _context_stream(ctx, norm_g[layer, 0], mod_ctx[0], mod_ctx[1], mod_ctx[2], ffn0,
                                    norm_g[layer, 1], mod_ctx[3], mod_ctx[4],
                                    w[:, q_end:k_end].astype(BF16), w[:, k_end:].T.astype(BF16))

    y_diff = _diff_attention(qT, k_lat, vT4, k_ctx, vT_ctx, diff_lambda[layer].astype(F32),
                             diff_subln_g[layer])
    conv_hy = _hyena_long_conv(u_hy, filt_w1[layer], filt_b1[layer], filt_w_inner[layer],
                               filt_b_inner[layer], filt_sin_freq[layer], filt_w_out[layer])

    wo = w_out[layer].astype(BF16)
    mixer = (conv_hy, x0_hy, u_hy, hyena_bias[layer], y_diff, wo[:D_HYENA], wo[D_HYENA:], mod_lat[5])
    return _ffn_half_step(h_lat, norm_g[layer, 2], mod_lat[6], mod_lat[7], mod_lat[8],
                          ffn_w + ((layer, 1),), mixer=mixer, final_g=final_g)
```
